```python
import math
import jax, jax.numpy as jnp
from jax import lax
import numpy as np

D_MODEL = 1024
BATCH = 16
SEQ = 2048
DEPTH = 4

N_MIXERS = 2
N_A_LAYERS = (DEPTH + N_MIXERS - 1) // N_MIXERS
N_B_LAYERS = DEPTH // N_MIXERS
A_HEADS = D_MODEL // 128
A_DHEAD = D_MODEL // (2 * A_HEADS)
A_VDIM = 2 * A_DHEAD
Q_BLOCK = 128
R_HEADS = max(4, D_MODEL // 256)
R_DK = D_MODEL // R_HEADS
R_DV = 2 * R_DK
CHUNK = 128
D_FF = 4 * D_MODEL
PLE_DIM = 256
EPS = 1e-6

kernel_name = "hybrid_diffattn_retnet_sqrelu_ple"


def rms_norm(x, g):
    xf = x.astype(jnp.float32)
    y = xf * lax.rsqrt(jnp.mean(xf * xf, axis=-1, keepdims=True) + EPS)
    return (y * g.astype(jnp.float32)).astype(x.dtype)


def alibi_slopes(n):
    return jnp.array([2.0 ** (-8.0 * (h + 1) / n) for h in range(n)], dtype=jnp.float32)


def diff_attention(xn, w_qkv, w_o, g_q, g_k, lam_q1, lam_k1, lam_q2, lam_k2, g_sub, lambda_init):
    B, S, _ = xn.shape
    qkv = xn @ w_qkv
    q = qkv[..., :D_MODEL].reshape(B, S, A_HEADS, 2, A_DHEAD)
    k = qkv[..., D_MODEL:2 * D_MODEL].reshape(B, S, A_HEADS, 2, A_DHEAD)
    v = qkv[..., 2 * D_MODEL:].reshape(B, S, A_HEADS, A_VDIM)
    q = rms_norm(q, g_q) * (A_DHEAD ** -0.5)
    k = rms_norm(k, g_k)
    lam = (jnp.exp(jnp.sum(lam_q1.astype(jnp.float32) * lam_k1.astype(jnp.float32)))
           - jnp.exp(jnp.sum(lam_q2.astype(jnp.float32) * lam_k2.astype(jnp.float32)))
           + lambda_init)
    slopes = alibi_slopes(A_HEADS)
    nb = S // Q_BLOCK
    q_blocks = q.reshape(B, nb, Q_BLOCK, A_HEADS, 2, A_DHEAD).transpose(1, 0, 2, 3, 4, 5)
    key_pos = jnp.arange(S)

    def one_block(args):
        qb, bi = args
        q_pos = bi * Q_BLOCK + jnp.arange(Q_BLOCK)
        s = jnp.einsum('bqhcd,bkhcd->bhcqk', qb, k).astype(jnp.float32)
        dist = q_pos[:, None] - key_pos[None, :]
        bias = -slopes[:, None, None] * dist.astype(jnp.float32)
        s = s + bias[None, :, None]
        s = jnp.where(dist[None, None, None] >= 0, s, -jnp.inf)
        a = jax.nn.softmax(s, axis=-1)
        attn = a[:, :, 0] - lam * a[:, :, 1]
        return jnp.einsum('bhqk,bkhe->bqhe', attn.astype(v.dtype), v)

    o = lax.map(one_block, (q_blocks, jnp.arange(nb)))
    o = o.transpose(1, 0, 2, 3, 4).reshape(B, S, A_HEADS, A_VDIM)
    o = rms_norm(o, g_sub) * (1.0 - lambda_init)
    return o.reshape(B, S, A_HEADS * A_VDIM) @ w_o


def retention(xn, w_in, w_out, g_gn):
    B, S, _ = xn.shape
    dt = xn.dtype
    proj = xn @ w_in
    qw, kw, vw = R_HEADS * R_DK, R_HEADS * R_DK, R_HEADS * R_DV
    q = proj[..., :qw]
    k = proj[..., qw:qw + kw] * (R_DK ** -0.5)
    v = proj[..., qw + kw:qw + kw + vw]
    gate = proj[..., qw + kw + vw:]
    N = S // CHUNK
    qc = q.reshape(B, N, CHUNK, R_HEADS, R_DK)
    kc = k.reshape(B, N, CHUNK, R_HEADS, R_DK)
    vc = v.reshape(B, N, CHUNK, R_HEADS, R_DV)

    gamma = 1.0 - 2.0 ** (-5.0 - jnp.arange(R_HEADS, dtype=jnp.float32))
    log_g = jnp.log(gamma)
    pos = jnp.arange(CHUNK, dtype=jnp.float32)
    diff = pos[:, None] - pos[None, :]
    intra_decay = jnp.where(diff[None] >= 0,
                            jnp.exp(jnp.maximum(diff, 0.0)[None] * log_g[:, None, None]),
                            0.0).astype(dt)
    q_dec = jnp.exp((pos[:, None] + 1.0) * log_g[None, :]).astype(dt)
    k_dec = jnp.exp((CHUNK - 1.0 - pos[:, None]) * log_g[None, :]).astype(dt)
    chunk_dec = jnp.exp(CHUNK * log_g).astype(dt)

    sc = jnp.einsum('bnihd,bnjhd->bnhij', qc, kc) * intra_decay[None, None]
    intra = jnp.einsum('bnhij,bnjhe->bnihe', sc, vc)

    def step(R, xs):
        q_n, k_n, v_n = xs
        cross = jnp.einsum('bihd,bhde->bihe', q_n, R) * q_dec[None, :, :, None]
        R_new = (chunk_dec[None, :, None, None] * R
                 + jnp.einsum('bjhd,bjhe->bhde', k_n * k_dec[None, :, :, None], v_n))
        return R_new, cross

    R0 = jnp.zeros((B, R_HEADS, R_DK, R_DV), dtype=dt)
    _, cross = lax.scan(step, R0, (qc.transpose(1, 0, 2, 3, 4),
                                   kc.transpose(1, 0, 2, 3, 4),
                                   vc.transpose(1, 0, 2, 3, 4)))
    o = intra + cross.transpose(1, 0, 2, 3, 4)
    o = rms_norm(o.reshape(B, S, R_HEADS, R_DV), g_gn).reshape(B, S, R_HEADS * R_DV)
    return (jax.nn.silu(gate) * o) @ w_out


def sq_relu_mlp(xn, w1, w2):
    h = jax.nn.relu(xn @ w1)
    return (h * h) @ w2


def setup_inputs(seed: int = 0) -> dict:
    key = jax.random.key(seed)
    ks = jax.random.split(key, 24)
    f32 = jnp.float32

    def nrm(k, shape, scale):
        return jax.random.normal(k, shape, f32) * scale

    def gain(k, shape):
        return 1.0 + 0.02 * jax.random.normal(k, shape, f32)

    return {
        "x": nrm(ks[0], (BATCH, SEQ, D_MODEL), 1.0),
        "p": nrm(ks[1], (DEPTH, BATCH, SEQ, PLE_DIM), 1.0),
        "norm_mix": gain(ks[2], (DEPTH, D_MODEL)),
        "norm_mlp": gain(ks[3], (DEPTH, D_MODEL)),
        "norm_pe": gain(ks[4], (DEPTH, D_MODEL)),
        "a_w_qkv": nrm(ks[5], (N_A_LAYERS, D_MODEL, 3 * D_MODEL), D_MODEL ** -0.5),
        "a_w_o": nrm(ks[6], (N_A_LAYERS, A_HEADS * A_VDIM, D_MODEL), (A_HEADS * A_VDIM) ** -0.5),
        "a_g_q": gain(ks[7], (N_A_LAYERS, A_DHEAD)),
        "a_g_k": gain(ks[8], (N_A_LAYERS, A_DHEAD)),
        "a_lam_q1": nrm(ks[9], (N_A_LAYERS, A_DHEAD), 0.1),
        "a_lam_k1": nrm(ks[10], (N_A_LAYERS, A_DHEAD), 0.1),
        "a_lam_q2": nrm(ks[11], (N_A_LAYERS, A_DHEAD), 0.1),
        "a_lam_k2": nrm(ks[12], (N_A_LAYERS, A_DHEAD), 0.1),
        "a_g_sub": gain(ks[13], (N_A_LAYERS, A_VDIM)),
        "r_w_in": nrm(ks[14], (N_B_LAYERS, D_MODEL, 2 * R_HEADS * R_DK + 2 * R_HEADS * R_DV), D_MODEL ** -0.5),
        "r_w_out": nrm(ks[15], (N_B_LAYERS, R_HEADS * R_DV, D_MODEL), (R_HEADS * R_DV) ** -0.5),
        "r_g_gn": gain(ks[16], (N_B_LAYERS, R_DV)),
        "mlp_w1": nrm(ks[17], (DEPTH, D_MODEL, D_FF), D_MODEL ** -0.5),
        "mlp_w2": nrm(ks[18], (DEPTH, D_FF, D_MODEL), 0.5 * D_FF ** -0.5),
        "pe_w_up": nrm(ks[19], (DEPTH, PLE_DIM, D_MODEL), PLE_DIM ** -0.5),
        "pe_w_gate": nrm(ks[20], (DEPTH, D_MODEL, D_MODEL), D_MODEL ** -0.5),
    }


def reference(x, p, norm_mix, norm_mlp, norm_pe,
              a_w_qkv, a_w_o, a_g_q, a_g_k, a_lam_q1, a_lam_k1, a_lam_q2, a_lam_k2, a_g_sub,
              r_w_in, r_w_out, r_g_gn,
              mlp_w1, mlp_w2, pe_w_up, pe_w_gate):
    for i in range(DEPTH):
        j = i // N_MIXERS
        h = rms_norm(x, norm_mix[i])
        if i % N_MIXERS == 0:
            lambda_init = 0.8 - 0.6 * math.exp(-0.3 * i)
            mix = diff_attention(h, a_w_qkv[j], a_w_o[j], a_g_q[j], a_g_k[j],
                                 a_lam_q1[j], a_lam_k1[j], a_lam_q2[j], a_lam_k2[j],
                                 a_g_sub[j], lambda_init)
        else:
            mix = retention(h, r_w_in[j], r_w_out[j], r_g_gn[j])
        x = x + mix
        x = x + sq_relu_mlp(rms_norm(x, norm_mlp[i]), mlp_w1[i], mlp_w2[i])
        gate = jax.nn.sigmoid(rms_norm(x, norm_pe[i]) @ pe_w_gate[i])
        x = x + gate * (p[i] @ pe_w_up[i])
    return x
```

```python
import functools
import math

import jax
import jax.numpy as jnp
from jax import lax
from jax.experimental import pallas as pl
from jax.experimental.pallas import tpu as pltpu

F32 = jnp.float32
BF16 = jnp.bfloat16

EPS = 1e-6
D_MODEL = 1024
DEPTH = 4
N_MIXERS = 2
A_HEADS = 8
A_DHEAD = 64
A_VDIM = 128
R_HEADS = 4
R_DK = 256
R_DV = 512
D_FF = 4096
PLE_DIM = 256

V7X_VMEM_BYTES = 64 * 1024 * 1024
VMEM_LIMIT = V7X_VMEM_BYTES - 8 * 1024 * 1024

TOKEN_TILE = 512
ATTN_TQ = 256
ATTN_TK = 256
RET_CHUNK = 256
FF_CHUNK = 1024
NORM_GROUP_TILE = 256
NEG_BIG = -1e30


def _resident(shape):
    nd = len(shape)
    return pl.BlockSpec(shape, lambda *_: (0,) * nd, pipeline_mode=pl.Buffered(1))


def _rms(x, g):
    return x * lax.rsqrt(jnp.mean(x * x, axis=-1, keepdims=True) + EPS) * g


def _dot(a, b):
    return jnp.dot(a, b, preferred_element_type=F32)


def _dot_nt(a, b):
    return lax.dot_general(a, b, (((1,), (1,)), ((), ())), preferred_element_type=F32)


def _dot_tn(a, b):
    return lax.dot_general(a, b, (((0,), (0,)), ((), ())), preferred_element_type=F32)


def _attn_inproj_kernel(x_ref, g_ref, wq_ref, wk_ref, wv_ref, gq_ref, gk_ref, gm_ref,
                        q_ref, k_ref, v_ref):
    xn = _rms(x_ref[...], g_ref[...]).astype(BF16)

    def qk_norm(w_ref, gain_ref, o_ref):
        y = _dot(xn, w_ref[...])
        for s in range(D_MODEL // NORM_GROUP_TILE):
            sl = slice(s * NORM_GROUP_TILE, (s + 1) * NORM_GROUP_TILE)
            ys = y[:, sl]
            ms = _dot((ys * ys).astype(BF16), gm_ref[...])
            o_ref[:, sl] = (ys * lax.rsqrt(ms + EPS) * gain_ref[:, sl]).astype(BF16)

    qk_norm(wq_ref, gq_ref, q_ref)
    qk_norm(wk_ref, gk_ref, k_ref)
    v_ref[...] = _dot(xn, wv_ref[...]).astype(BF16)


def _attn_inproj(x, g, wq, wk, wv, gq_row, gk_row, gmat):
    T = x.shape[0]
    tm = TOKEN_TILE
    row = pl.BlockSpec((tm, D_MODEL), lambda i: (i, 0))
    out = jax.ShapeDtypeStruct((T, D_MODEL), BF16)
    return pl.pallas_call(
        _attn_inproj_kernel,
        grid=(T // tm,),
        in_specs=[row, _resident((1, D_MODEL)),
                  _resident((D_MODEL, D_MODEL)), _resident((D_MODEL, D_MODEL)),
                  _resident((D_MODEL, D_MODEL)),
                  _resident((1, D_MODEL)), _resident((1, D_MODEL)),
                  _resident((NORM_GROUP_TILE, NORM_GROUP_TILE))],
        out_specs=[row, row, row],
        out_shape=[out, out, out],
        compiler_params=pltpu.CompilerParams(
            dimension_semantics=("parallel",), vmem_limit_bytes=VMEM_LIMIT),
        name="attn_inproj",
    )(x, g, wq, wk, wv, gq_row, gk_row, gmat)


def _attn_core_kernel(lambda_init, slopes_ref, q_ref, k_ref, v_ref, bias_ref, bias_diag_ref,
                      lq1_ref, lk1_ref, lq2_ref, lk2_ref, gsub_ref, o_ref):
    tq, tk = ATTN_TQ, ATTN_TK
    h = pl.program_id(1)
    qi = pl.program_id(2)
    slope = slopes_ref[h]

    q = q_ref[0]
    lane = lax.broadcasted_iota(jnp.int32, q.shape, 1)
    zero = jnp.zeros_like(q)
    qc = (jnp.where(lane < A_DHEAD, q, zero), jnp.where(lane >= A_DHEAD, q, zero))

    def tile_update(carry, k, v, bias, off):
        new = []
        for c in range(2):
            m, l, acc = carry[c]
            s = _dot_nt(qc[c], k) + bias
            m_new = jnp.maximum(m, jnp.max(s, axis=-1, keepdims=True) + off)
            p = jnp.exp(s - (m_new - off))
            alpha = jnp.exp(m - m_new)
            l_new = alpha * l + jnp.sum(p, axis=-1, keepdims=True)
            acc_new = alpha * acc + _dot(p.astype(BF16), v)
            new.append((m_new, l_new, acc_new))
        return tuple(new)

    def body(kj, carry):
        start = pl.multiple_of(kj * tk, tk)
        k = k_ref[0, pl.ds(start, tk), :]
        v = v_ref[0, pl.ds(start, tk), :]
        off = -slope * ((qi - kj) * tq).astype(F32)
        return tile_update(carry, k, v, bias_ref[0], off)

    init_c = (jnp.full((tq, 1), NEG_BIG, F32), jnp.zeros((tq, 1), F32),
              jnp.zeros((tq, A_VDIM), F32))
    carry = lax.fori_loop(0, qi, body, (init_c, init_c))

    start = pl.multiple_of(qi * tk, tk)
    carry = tile_update(carry, k_ref[0, pl.ds(start, tk), :], v_ref[0, pl.ds(start, tk), :],
                        bias_diag_ref[0], jnp.float32(0.0))

    lam = (jnp.exp(jnp.sum(lq1_ref[...] * lk1_ref[...], keepdims=True))
           - jnp.exp(jnp.sum(lq2_ref[...] * lk2_ref[...], keepdims=True))
           + lambda_init)
    (_, l1, a1), (_, l2, a2) = carry
    o = a1 / l1 - lam * (a2 / l2)
    o_ref[0] = (_rms(o, gsub_ref[...]) * (1.0 - lambda_init)).astype(BF16)


def _attn_core(q, k, v, slopes, bias, bias_diag, lq1, lk1, lq2, lk2, gsub, lambda_init):
    B, S, _ = q.shape
    tq, tk = ATTN_TQ, ATTN_TK
    qspec = pl.BlockSpec((1, tq, A_VDIM), lambda b, h, i: (b, i, h))
    kvspec = pl.BlockSpec((1, S, A_VDIM), lambda b, h, i: (b, 0, h))
    bspec = pl.BlockSpec((1, tq, tk), lambda b, h, i: (h, 0, 0))
    vec = lambda n: pl.BlockSpec((1, n), lambda b, h, i: (0, 0))
    return pl.pallas_call(
        functools.partial(_attn_core_kernel, lambda_init),
        grid=(B, A_HEADS, S // tq),
        in_specs=[pl.BlockSpec(memory_space=pltpu.SMEM),
                  qspec, kvspec, kvspec, bspec, bspec,
                  vec(A_DHEAD), vec(A_DHEAD), vec(A_DHEAD), vec(A_DHEAD), vec(A_VDIM)],
        out_specs=qspec,
        out_shape=jax.ShapeDtypeStruct((B, S, A_HEADS * A_VDIM), BF16),
        compiler_params=pltpu.CompilerParams(
            dimension_semantics=("parallel", "parallel", "parallel"),
            vmem_limit_bytes=VMEM_LIMIT),
        name="attn_core",
    )(slopes, q, k, v, bias, bias_diag, lq1, lk1, lq2, lk2, gsub)


def _ret_inproj_kernel(x_ref, g_ref, w_ref, o_ref):
    xn = _rms(x_ref[...], g_ref[...]).astype(BF16)
    n_out = o_ref.shape[1]
    for c in range(n_out // D_MODEL):
        sl = slice(c * D_MODEL, (c + 1) * D_MODEL)
        o_ref[:, sl] = _dot(xn, w_ref[:, sl]).astype(BF16)


def _ret_inproj(x, g, w):
    T = x.shape[0]
    n_out = w.shape[1]
    tm = TOKEN_TILE
    return pl.pallas_call(
        _ret_inproj_kernel,
        grid=(T // tm,),
        in_specs=[pl.BlockSpec((tm, D_MODEL), lambda i: (i, 0)),
                  _resident((1, D_MODEL)), _resident((D_MODEL, n_out))],
        out_specs=pl.BlockSpec((tm, n_out), lambda i: (i, 0)),
        out_shape=jax.ShapeDtypeStruct((T, n_out), BF16),
        compiler_params=pltpu.CompilerParams(
            dimension_semantics=("parallel",), vmem_limit_bytes=VMEM_LIMIT),
        name="ret_inproj",
    )(x, g, w)


def _ret_core_kernel(cdec_ref, q_ref, k_ref, v_ref, gate_ref, dec_ref, qdec_ref, kdec_ref,
                     ggn_ref, o_ref, state_ref):
    h = pl.program_id(1)

    @pl.when(pl.program_id(2) == 0)
    def _():
        state_ref[...] = jnp.zeros_like(state_ref)

    q, k, v = q_ref[0], k_ref[0], v_ref[0]
    state = state_ref[...]
    sc = _dot_nt(q, k) * dec_ref[0]
    o = _dot(sc.astype(BF16), v) + _dot(q, state.astype(BF16)) * qdec_ref[0]
    kd = (k.astype(F32) * kdec_ref[0]).astype(BF16)
    state_ref[...] = cdec_ref[h] * state + _dot_tn(kd, v)
    gate = gate_ref[0].astype(F32)
    o_ref[0] = (gate * jax.nn.sigmoid(gate) * _rms(o, ggn_ref[...])).astype(BF16)


def _ret_core(proj, cdec, dec, qdec, kdec, ggn):
    B, S, _ = proj.shape
    C = RET_CHUNK
    kq = R_HEADS * R_DK // R_DK
    kv = 2 * R_HEADS * R_DK // R_DV
    kg = kv + R_HEADS
    tab = lambda w: pl.BlockSpec((1, C, w), lambda b, h, n: (h, 0, 0))
    return pl.pallas_call(
        _ret_core_kernel,
        grid=(B, R_HEADS, S // C),
        in_specs=[pl.BlockSpec(memory_space=pltpu.SMEM),
                  pl.BlockSpec((1, C, R_DK), lambda b, h, n: (b, n, h)),
                  pl.BlockSpec((1, C, R_DK), lambda b, h, n: (b, n, kq + h)),
                  pl.BlockSpec((1, C, R_DV), lambda b, h, n: (b, n, kv + h)),
                  pl.BlockSpec((1, C, R_DV), lambda b, h, n: (b, n, kg + h)),
                  tab(C), tab(1), tab(1),
                  pl.BlockSpec((1, R_DV), lambda b, h, n: (0, 0))],
        out_specs=pl.BlockSpec((1, C, R_DV), lambda b, h, n: (b, n, h)),
        out_shape=jax.ShapeDtypeStruct((B, S, R_HEADS * R_DV), BF16),
        scratch_shapes=[pltpu.VMEM((R_DK, R_DV), F32)],
        compiler_params=pltpu.CompilerParams(
            dimension_semantics=("parallel", "parallel", "arbitrary"),
            vmem_limit_bytes=VMEM_LIMIT),
        name="ret_core",
    )(cdec, proj, proj, proj, proj, dec, qdec, kdec, ggn)


def _tail_kernel(x_ref, y_ref, p_ref, wo_ref, gmlp_ref, w1_ref, w2_ref, gpe_ref, wg_ref,
                 wup_ref, o_ref):
    x = x_ref[...] + _dot(y_ref[...], wo_ref[...])
    xn = _rms(x, gmlp_ref[...]).astype(BF16)
    acc = x
    for c in range(D_FF // FF_CHUNK):
        sl = slice(c * FF_CHUNK, (c + 1) * FF_CHUNK)
        hid = jnp.maximum(_dot(xn, w1_ref[:, sl]), 0.0)
        acc = acc + _dot((hid * hid).astype(BF16), w2_ref[sl, :])
    x = acc
    gate = jax.nn.sigmoid(_dot(_rms(x, gpe_ref[...]).astype(BF16), wg_ref[...]))
    o_ref[...] = x + gate * _dot(p_ref[...].astype(BF16), wup_ref[...])


def _tail(x, y, p, wo, gmlp, w1, w2, gpe, wg, wup):
    T = x.shape[0]
    dy = y.shape[1]
    tm = TOKEN_TILE
    row = pl.BlockSpec((tm, D_MODEL), lambda i: (i, 0))
    return pl.pallas_call(
        _tail_kernel,
        grid=(T // tm,),
        in_specs=[row,
                  pl.BlockSpec((tm, dy), lambda i: (i, 0)),
                  pl.BlockSpec((tm, PLE_DIM), lambda i: (i, 0)),
                  _resident((dy, D_MODEL)), _resident((1, D_MODEL)),
                  _resident((D_MODEL, D_FF)), _resident((D_FF, D_MODEL)),
                  _resident((1, D_MODEL)), _resident((D_MODEL, D_MODEL)),
                  _resident((PLE_DIM, D_MODEL))],
        out_specs=row,
        out_shape=jax.ShapeDtypeStruct((T, D_MODEL), F32),
        compiler_params=pltpu.CompilerParams(
            dimension_semantics=("parallel",), vmem_limit_bytes=VMEM_LIMIT),
        name="layer_tail",
    )(x, y, p, wo, gmlp, w1, w2, gpe, wg, wup)


def _attn_tables():
    slopes = jnp.array([2.0 ** (-8.0 * (h + 1) / A_HEADS) for h in range(A_HEADS)], F32)
    r = jnp.arange(ATTN_TQ, dtype=F32)[:, None]
    c = jnp.arange(ATTN_TK, dtype=F32)[None, :]
    bias = -slopes[:, None, None] * (r - c)[None]
    bias_diag = jnp.where((r >= c)[None], bias, NEG_BIG)
    gmat = jnp.where(jnp.arange(NORM_GROUP_TILE)[:, None] // A_DHEAD
                     == jnp.arange(NORM_GROUP_TILE)[None, :] // A_DHEAD,
                     1.0 / A_DHEAD, 0.0).astype(BF16)
    return slopes, bias, bias_diag, gmat


def _ret_tables():
    C = RET_CHUNK
    scale = R_DK ** -0.5
    gamma = 1.0 - 2.0 ** (-5.0 - jnp.arange(R_HEADS, dtype=F32))
    log_g = jnp.log(gamma)
    pos = jnp.arange(C, dtype=F32)
    diff = pos[:, None] - pos[None, :]
    dec = jnp.where(diff[None] >= 0,
                    jnp.exp(jnp.maximum(diff, 0.0)[None] * log_g[:, None, None]), 0.0) * scale
    qdec = jnp.exp((pos[None, :] + 1.0) * log_g[:, None])[..., None]
    kdec = (jnp.exp((C - 1.0 - pos[None, :]) * log_g[:, None]) * scale)[..., None]
    cdec = jnp.exp(C * log_g)
    return cdec, dec, qdec, kdec


def kernel(x, p, norm_mix, norm_mlp, norm_pe, a_w_qkv, a_w_o, a_g_q, a_g_k, a_lam_q1, a_lam_k1,
           a_lam_q2, a_lam_k2, a_g_sub, r_w_in, r_w_out, r_g_gn, mlp_w1, mlp_w2, pe_w_up,
           pe_w_gate):
    B, S, _ = x.shape
    T = B * S
    slopes, bias, bias_diag, gmat = _attn_tables()
    cdec, dec, qdec, kdec = _ret_tables()
    row = lambda a: a.reshape(1, -1).astype(F32)

    xt = x.reshape(T, D_MODEL)
    for i in range(DEPTH):
        j = i // N_MIXERS
        if i % N_MIXERS == 0:
            lambda_init = 0.8 - 0.6 * math.exp(-0.3 * i)
            w = a_w_qkv[j].astype(BF16)
            gq_row = row(jnp.tile(a_g_q[j], D_MODEL // A_DHEAD) * (A_DHEAD ** -0.5))
            gk_row = row(jnp.tile(a_g_k[j], D_MODEL // A_DHEAD))
            q, k, v = _attn_inproj(xt, row(norm_mix[i]), w[:, :D_MODEL],
                                   w[:, D_MODEL:2 * D_MODEL], w[:, 2 * D_MODEL:],
                                   gq_row, gk_row, gmat)
            shp = (B, S, D_MODEL)
            mix = _attn_core(q.reshape(shp), k.reshape(shp), v.reshape(shp), slopes, bias,
                             bias_diag, row(a_lam_q1[j]), row(a_lam_k1[j]), row(a_lam_q2[j]),
                             row(a_lam_k2[j]), row(a_g_sub[j]), lambda_init)
            w_out = a_w_o[j]
        else:
            proj = _ret_inproj(xt, row(norm_mix[i]), r_w_in[j].astype(BF16))
            mix = _ret_core(proj.reshape(B, S, -1), cdec, dec, qdec, kdec, row(r_g_gn[j]))
            w_out = r_w_out[j]
        xt = _tail(xt, mix.reshape(T, -1), p[i].reshape(T, PLE_DIM), w_out.astype(BF16),
                   row(norm_mlp[i]), mlp_w1[i].astype(BF16), mlp_w2[i].astype(BF16),
                   row(norm_pe[i]), pe_w_gate[i].astype(BF16), pe_w_up[i].astype(BF16))
    return xt.reshape(B, S, D_MODEL)
```

```python
import functools
import math

import jax
import jax.numpy as jnp
from jax import lax
from jax.experimental import pallas as pl
from jax.experimental.pallas import tpu as pltpu

F32 = jnp.float32
BF16 = jnp.bfloat16

EPS = 1e-6
D_MODEL = 1024
DEPTH = 4
N_MIXERS = 2
A_HEADS = 8
A_DHEAD = 64
A_VDIM = 128
R_HEADS = 4
R_DK = 256
R_DV = 512
D_FF = 4096
PLE_DIM = 256

V7X_VMEM_BYTES = 64 * 1024 * 1024
VMEM_LIMIT = V7X_VMEM_BYTES - 8 * 1024 * 1024

TOKEN_TILE = 512
ATTN_TQ = 256
ATTN_TK = 256
RET_CHUNK = 256
FF_CHUNK = 1024
NORM_GROUP_TILE = 256
NEG_BIG = -1e30
LANES = 128
ONES_ROWS = 16
LOG2E = math.log2(math.e)


def _resident(shape):
    nd = len(shape)
    return pl.BlockSpec(shape, lambda *_: (0,) * nd, pipeline_mode=pl.Buffered(1))


def _rms(x, g):
    return x * lax.rsqrt(jnp.mean(x * x, axis=-1, keepdims=True) + EPS) * g


def _dot(a, b):
    return jnp.dot(a, b, preferred_element_type=F32)


def _dot_nt(a, b):
    return lax.dot_general(a, b, (((1,), (1,)), ((), ())), preferred_element_type=F32)


def _dot_tn(a, b):
    return lax.dot_general(a, b, (((0,), (0,)), ((), ())), preferred_element_type=F32)


def _attn_inproj_kernel(x_ref, g_ref, wq_ref, wk_ref, wv_ref, gq_ref, gk_ref, gm_ref,
                        q_ref, k_ref, v_ref):
    xn = _rms(x_ref[...], g_ref[...]).astype(BF16)

    def qk_norm(w_ref, gain_ref, o_ref):
        y = _dot(xn, w_ref[...])
        for s in range(D_MODEL // NORM_GROUP_TILE):
            sl = slice(s * NORM_GROUP_TILE, (s + 1) * NORM_GROUP_TILE)
            ys = y[:, sl]
            ms = _dot((ys * ys).astype(BF16), gm_ref[...])
            o_ref[:, sl] = (ys * lax.rsqrt(ms + EPS) * gain_ref[:, sl]).astype(BF16)

    qk_norm(wq_ref, gq_ref, q_ref)
    qk_norm(wk_ref, gk_ref, k_ref)
    v_ref[...] = _dot(xn, wv_ref[...]).astype(BF16)


def _attn_inproj(x, g, wq, wk, wv, gq_row, gk_row, gmat):
    T = x.shape[0]
    tm = TOKEN_TILE
    row = pl.BlockSpec((tm, D_MODEL), lambda i: (i, 0))
    out = jax.ShapeDtypeStruct((T, D_MODEL), BF16)
    return pl.pallas_call(
        _attn_inproj_kernel,
        grid=(T // tm,),
        in_specs=[row, _resident((1, D_MODEL)),
                  _resident((D_MODEL, D_MODEL)), _resident((D_MODEL, D_MODEL)),
                  _resident((D_MODEL, D_MODEL)),
                  _resident((1, D_MODEL)), _resident((1, D_MODEL)),
                  _resident((NORM_GROUP_TILE, NORM_GROUP_TILE))],
        out_specs=[row, row, row],
        out_shape=[out, out, out],
        compiler_params=pltpu.CompilerParams(
            dimension_semantics=("parallel",), vmem_limit_bytes=VMEM_LIMIT),
        name="attn_inproj",
    )(x, g, wq, wk, wv, gq_row, gk_row, gmat)


def _attn_core_kernel(lambda_init, slopes_ref, q_ref, k_ref, v_ref, bias_ref,
                      lq1_ref, lk1_ref, lq2_ref, lk2_ref, gsub_ref, o_ref,
                      qcat_ref, m_ref, acc_ref):
    tq, tk = ATTN_TQ, ATTN_TK
    qi = pl.program_id(1)

    q = q_ref[0]
    lane = lax.broadcasted_iota(jnp.int32, (tq, A_VDIM), 1)
    for h in range(A_HEADS):
        qh = q[:, h * A_VDIM:(h + 1) * A_VDIM]
        zero = jnp.zeros_like(qh)
        qcat_ref[h, :tq, :] = jnp.where(lane < A_DHEAD, qh, zero)
        qcat_ref[h, tq:, :] = jnp.where(lane >= A_DHEAD, qh, zero)
    m_ref[...] = jnp.full_like(m_ref, NEG_BIG)
    acc_ref[...] = jnp.zeros_like(acc_ref)

    def body(kj, _):
        start = pl.multiple_of(kj * tk, tk)
        diag = jnp.where(kj == qi, 1, 0)
        dist = ((qi - kj) * tq).astype(F32)
        def scores(h):
            k = k_ref[0, pl.ds(start, tk), h * A_VDIM:(h + 1) * A_VDIM]
            return _dot_nt(k, qcat_ref[h])

        s_next = scores(0)
        for h in range(A_HEADS):
            s = s_next
            if h + 1 < A_HEADS:
                s_next = scores(h + 1)
            v = v_ref[0, pl.ds(start, tk), h * A_VDIM:(h + 1) * A_VDIM]
            off = -slopes_ref[h] * dist
            bias = bias_ref[diag * A_HEADS + h]
            m = m_ref[h]
            m_new, p = [], []
            for cb in range(2 * tq // LANES):
                cs = slice(cb * LANES, (cb + 1) * LANES)
                bs = slice((cb * LANES) % tq, (cb * LANES) % tq + LANES)
                sb = s[:, cs] + bias[:, bs]
                mb = jnp.maximum(m[:, cs], jnp.max(sb, axis=0, keepdims=True) + off)
                p.append(jnp.exp2(sb - (mb - off)).astype(BF16))
                m_new.append(mb)
            m_new = jnp.concatenate(m_new, axis=1)
            alpha = jnp.exp2(m - m_new)
            v_aug = jnp.concatenate([v.T, jnp.ones((ONES_ROWS, tk), BF16)], axis=0)
            acc_ref[h] = alpha * acc_ref[h] + _dot(v_aug, jnp.concatenate(p, axis=1))
            m_ref[h] = m_new
        return 0

    lax.fori_loop(0, qi + 1, body, 0)

    lam = (jnp.exp(jnp.sum(lq1_ref[...] * lk1_ref[...], keepdims=True))
           - jnp.exp(jnp.sum(lq2_ref[...] * lk2_ref[...], keepdims=True))
           + lambda_init)
    for h in range(A_HEADS):
        acc = acc_ref[h]
        a = acc[:A_VDIM] / acc[A_VDIM:A_VDIM + 1]
        o = a[:, :tq] - lam * a[:, tq:]
        o = o * lax.rsqrt(jnp.mean(o * o, axis=0, keepdims=True) + EPS) * gsub_ref[...]
        o_ref[0, :, h * A_VDIM:(h + 1) * A_VDIM] = (o * (1.0 - lambda_init)).T.astype(BF16)


def _attn_core(q, k, v, slopes, bias, lq1, lk1, lq2, lk2, gsub_col, lambda_init):
    B, S, W = q.shape
    tq, tk = ATTN_TQ, ATTN_TK
    qspec = pl.BlockSpec((1, tq, W), lambda b, i: (b, i, 0))
    kvspec = pl.BlockSpec((1, S, W), lambda b, i: (b, 0, 0))
    vec = lambda n: pl.BlockSpec((1, n), lambda b, i: (0, 0))
    return pl.pallas_call(
        functools.partial(_attn_core_kernel, lambda_init),
        grid=(B, S // tq),
        in_specs=[pl.BlockSpec(memory_space=pltpu.SMEM),
                  qspec, kvspec, kvspec, _resident(bias.shape),
                  vec(A_DHEAD), vec(A_DHEAD), vec(A_DHEAD), vec(A_DHEAD),
                  pl.BlockSpec((A_VDIM, 1), lambda b, i: (0, 0))],
        out_specs=qspec,
        out_shape=jax.ShapeDtypeStruct((B, S, W), BF16),
        scratch_shapes=[pltpu.VMEM((A_HEADS, 2 * tq, A_VDIM), BF16),
                        pltpu.VMEM((A_HEADS, 1, 2 * tq), F32),
                        pltpu.VMEM((A_HEADS, A_VDIM + ONES_ROWS, 2 * tq), F32)],
        compiler_params=pltpu.CompilerParams(
            dimension_semantics=("parallel", "parallel"), vmem_limit_bytes=VMEM_LIMIT),
        name="attn_core",
    )(slopes, q, k, v, bias, lq1, lk1, lq2, lk2, gsub_col)


def _ret_inproj_kernel(x_ref, g_ref, w_ref, o_ref):
    xn = _rms(x_ref[...], g_ref[...]).astype(BF16)
    n_out = o_ref.shape[1]
    for c in range(n_out // D_MODEL):
        sl = slice(c * D_MODEL, (c + 1) * D_MODEL)
        o_ref[:, sl] = _dot(xn, w_ref[:, sl]).astype(BF16)


def _ret_inproj(x, g, w):
    T = x.shape[0]
    n_out = w.shape[1]
    tm = TOKEN_TILE
    return pl.pallas_call(
        _ret_inproj_kernel,
        grid=(T // tm,),
        in_specs=[pl.BlockSpec((tm, D_MODEL), lambda i: (i, 0)),
                  _resident((1, D_MODEL)), _resident((D_MODEL, n_out))],
        out_specs=pl.BlockSpec((tm, n_out), lambda i: (i, 0)),
        out_shape=jax.ShapeDtypeStruct((T, n_out), BF16),
        compiler_params=pltpu.CompilerParams(
            dimension_semantics=("parallel",), vmem_limit_bytes=VMEM_LIMIT),
        name="ret_inproj",
    )(x, g, w)


def _ret_core_kernel(cdec_ref, q_ref, k_ref, v_ref, gate_ref, dec_ref, qdec_ref, kdec_ref,
                     ggn_ref, o_ref, state_ref):
    h = pl.program_id(1)

    @pl.when(pl.program_id(2) == 0)
    def _():
        state_ref[...] = jnp.zeros_like(state_ref)

    q, k, v = q_ref[0], k_ref[0], v_ref[0]
    state = state_ref[...]
    sc = _dot_nt(q, k) * dec_ref[0]
    o = _dot(sc.astype(BF16), v) + _dot(q, state.astype(BF16)) * qdec_ref[0]
    kd = (k.astype(F32) * kdec_ref[0]).astype(BF16)
    state_ref[...] = cdec_ref[h] * state + _dot_tn(kd, v)
    gate = gate_ref[0].astype(F32)
    o_ref[0] = (gate * jax.nn.sigmoid(gate) * _rms(o, ggn_ref[...])).astype(BF16)


def _ret_core(proj, cdec, dec, qdec, kdec, ggn):
    B, S, _ = proj.shape
    C = RET_CHUNK
    kq = R_HEADS * R_DK // R_DK
    kv = 2 * R_HEADS * R_DK // R_DV
    kg = kv + R_HEADS
    tab = lambda w: pl.BlockSpec((1, C, w), lambda b, h, n: (h, 0, 0))
    return pl.pallas_call(
        _ret_core_kernel,
        grid=(B, R_HEADS, S // C),
        in_specs=[pl.BlockSpec(memory_space=pltpu.SMEM),
                  pl.BlockSpec((1, C, R_DK), lambda b, h, n: (b, n, h)),
                  pl.BlockSpec((1, C, R_DK), lambda b, h, n: (b, n, kq + h)),
                  pl.BlockSpec((1, C, R_DV), lambda b, h, n: (b, n, kv + h)),
                  pl.BlockSpec((1, C, R_DV), lambda b, h, n: (b, n, kg + h)),
                  tab(C), tab(1), tab(1),
                  pl.BlockSpec((1, R_DV), lambda b, h, n: (0, 0))],
        out_specs=pl.BlockSpec((1, C, R_DV), lambda b, h, n: (b, n, h)),
        out_shape=jax.ShapeDtypeStruct((B, S, R_HEADS * R_DV), BF16),
        scratch_shapes=[pltpu.VMEM((R_DK, R_DV), F32)],
        compiler_params=pltpu.CompilerParams(
            dimension_semantics=("parallel", "parallel", "arbitrary"),
            vmem_limit_bytes=VMEM_LIMIT),
        name="ret_core",
    )(cdec, proj, proj, proj, proj, dec, qdec, kdec, ggn)


def _tail_kernel(x_ref, y_ref, p_ref, wo_ref, gmlp_ref, w1_ref, w2_ref, gpe_ref, wg_ref,
                 wup_ref, o_ref):
    x = x_ref[...] + _dot(y_ref[...], wo_ref[...])
    xn = _rms(x, gmlp_ref[...]).astype(BF16)
    acc = x
    for c in range(D_FF // FF_CHUNK):
        sl = slice(c * FF_CHUNK, (c + 1) * FF_CHUNK)
        hid = jnp.maximum(_dot(xn, w1_ref[:, sl]), 0.0)
        acc = acc + _dot((hid * hid).astype(BF16), w2_ref[sl, :])
    x = acc
    gate = jax.nn.sigmoid(_dot(_rms(x, gpe_ref[...]).astype(BF16), wg_ref[...]))
    o_ref[...] = x + gate * _dot(p_ref[...].astype(BF16), wup_ref[...])


def _tail(x, y, p, wo, gmlp, w1, w2, gpe, wg, wup):
    T = x.shape[0]
    dy = y.shape[1]
    tm = TOKEN_TILE
    row = pl.BlockSpec((tm, D_MODEL), lambda i: (i, 0))
    return pl.pallas_call(
        _tail_kernel,
        grid=(T // tm,),
        in_specs=[row,
                  pl.BlockSpec((tm, dy), lambda i: (i, 0)),
                  pl.BlockSpec((tm, PLE_DIM), lambda i: (i, 0)),
                  _resident((dy, D_MODEL)), _resident((1, D_MODEL)),
                  _resident((D_MODEL, D_FF)), _resident((D_FF, D_MODEL)),
                  _resident((1, D_MODEL)), _resident((D_MODEL, D_MODEL)),
                  _resident((PLE_DIM, D_MODEL))],
        out_specs=row,
        out_shape=jax.ShapeDtypeStruct((T, D_MODEL), F32),
        compiler_params=pltpu.CompilerParams(
            dimension_semantics=("parallel",), vmem_limit_bytes=VMEM_LIMIT),
        name="layer_tail",
    )(x, y, p, wo, gmlp, w1, w2, gpe, wg, wup)


def _attn_tables():
    slopes = jnp.array([2.0 ** (-8.0 * (h + 1) / A_HEADS) for h in range(A_HEADS)], F32)
    kpos = jnp.arange(ATTN_TK, dtype=F32)[:, None]
    qpos = jnp.arange(ATTN_TQ, dtype=F32)[None, :]
    slopes = slopes * LOG2E
    bias = -slopes[:, None, None] * (qpos - kpos)[None]
    bias = jnp.concatenate([bias, jnp.where((qpos >= kpos)[None], bias, NEG_BIG)], axis=0)
    gmat = jnp.where(jnp.arange(NORM_GROUP_TILE)[:, None] // A_DHEAD
                     == jnp.arange(NORM_GROUP_TILE)[None, :] // A_DHEAD,
                     1.0 / A_DHEAD, 0.0).astype(BF16)
    return slopes, bias, gmat


def _ret_tables():
    C = RET_CHUNK
    scale = R_DK ** -0.5
    gamma = 1.0 - 2.0 ** (-5.0 - jnp.arange(R_HEADS, dtype=F32))
    log_g = jnp.log(gamma)
    pos = jnp.arange(C, dtype=F32)
    diff = pos[:, None] - pos[None, :]
    dec = jnp.where(diff[None] >= 0,
                    jnp.exp(jnp.maximum(diff, 0.0)[None] * log_g[:, None, None]), 0.0) * scale
    qdec = jnp.exp((pos[None, :] + 1.0) * log_g[:, None])[..., None]
    kdec = (jnp.exp((C - 1.0 - pos[None, :]) * log_g[:, None]) * scale)[..., None]
    cdec = jnp.exp(C * log_g)
    return cdec, dec, qdec, kdec


def kernel(x, p, norm_mix, norm_mlp, norm_pe, a_w_qkv, a_w_o, a_g_q, a_g_k, a_lam_q1, a_lam_k1,
           a_lam_q2, a_lam_k2, a_g_sub, r_w_in, r_w_out, r_g_gn, mlp_w1, mlp_w2, pe_w_up,
           pe_w_gate):
    B, S, _ = x.shape
    T = B * S
    slopes, bias, gmat = _attn_tables()
    cdec, dec, qdec, kdec = _ret_tables()
    row = lambda a: a.reshape(1, -1).astype(F32)

    xt = x.reshape(T, D_MODEL)
    for i in range(DEPTH):
        j = i // N_MIXERS
        if i % N_MIXERS == 0:
            lambda_init = 0.8 - 0.6 * math.exp(-0.3 * i)
            w = a_w_qkv[j].astype(BF16)
            gq_row = row(jnp.tile(a_g_q[j], D_MODEL // A_DHEAD) * (A_DHEAD ** -0.5 * LOG2E))
            gk_row = row(jnp.tile(a_g_k[j], D_MODEL // A_DHEAD))
            q, k, v = _attn_inproj(xt, row(norm_mix[i]), w[:, :D_MODEL],
                                   w[:, D_MODEL:2 * D_MODEL], w[:, 2 * D_MODEL:],
                                   gq_row, gk_row, gmat)
            shp = (B, S, D_MODEL)
            mix = _attn_core(q.reshape(shp), k.reshape(shp), v.reshape(shp), slopes, bias,
                             row(a_lam_q1[j]), row(a_lam_k1[j]), row(a_lam_q2[j]),
                             row(a_lam_k2[j]), a_g_sub[j].reshape(-1, 1).astype(F32),
                             lambda_init)
            w_out = a_w_o[j]
        else:
            proj = _ret_inproj(xt, row(norm_mix[i]), r_w_in[j].astype(BF16))
            mix = _ret_core(proj.reshape(B, S, -1), cdec, dec, qdec, kdec, row(r_g_gn[j]))
            w_out = r_w_out[j]
        xt = _tail(xt, mix.reshape(T, -1), p[i].reshape(T, PLE_DIM), w_out.astype(BF16),
                   row(norm_mlp[i]), mlp_w1[i].astype(BF16), mlp_w2[i].astype(BF16),
                   row(norm_pe[i]), pe_w_gate[i].astype(BF16), pe_w_up[i].astype(BF16))
    return xt.reshape(B, S, D_MODEL)
```

```python
import functools
import math

import jax
import jax.numpy as jnp
from jax import lax
from jax.experimental import pallas as pl
from jax.experimental.pallas import tpu as pltpu

F32 = jnp.float32
BF16 = jnp.bfloat16

EPS = 1e-6
D_MODEL = 1024
DEPTH = 4
N_MIXERS = 2
A_HEADS = 8
A_DHEAD = 64
A_VDIM = 128
R_HEADS = 4
R_DK = 256
R_DV = 512
D_FF = 4096
PLE_DIM = 256

V7X_VMEM_BYTES = 64 * 1024 * 1024
VMEM_LIMIT = V7X_VMEM_BYTES - 8 * 1024 * 1024

TOKEN_TILE = 512
ATTN_TQ = 256
ATTN_TK = 256
RET_CHUNK = 256
FF_CHUNK = 1024
NORM_GROUP_TILE = 256
NEG_BIG = -1e30
LANES = 128
ONES_ROWS = 16
LOG2E = math.log2(math.e)


def _resident(shape):
    nd = len(shape)
    return pl.BlockSpec(shape, lambda *_: (0,) * nd, pipeline_mode=pl.Buffered(1))


def _rms(x, g):
    return x * lax.rsqrt(jnp.mean(x * x, axis=-1, keepdims=True) + EPS) * g


def _dot(a, b):
    return jnp.dot(a, b, preferred_element_type=F32)


def _dot_nt(a, b):
    return lax.dot_general(a, b, (((1,), (1,)), ((), ())), preferred_element_type=F32)


def _dot_tn(a, b):
    return lax.dot_general(a, b, (((0,), (0,)), ((), ())), preferred_element_type=F32)


def _attn_inproj_kernel(x_ref, g_ref, wq_ref, wk_ref, wv_ref, gq_ref, gk_ref, gm_ref,
                        q_ref, k_ref, v_ref):
    xn = _rms(x_ref[...], g_ref[...]).astype(BF16)

    def qk_norm(w_ref, gain_ref, o_ref):
        y = _dot(xn, w_ref[...])
        for s in range(D_MODEL // NORM_GROUP_TILE):
            sl = slice(s * NORM_GROUP_TILE, (s + 1) * NORM_GROUP_TILE)
            ys = y[:, sl]
            ms = _dot((ys * ys).astype(BF16), gm_ref[...])
            o_ref[:, sl] = (ys * lax.rsqrt(ms + EPS) * gain_ref[:, sl]).astype(BF16)

    qk_norm(wq_ref, gq_ref, q_ref)
    qk_norm(wk_ref, gk_ref, k_ref)
    v_ref[...] = _dot(xn, wv_ref[...]).astype(BF16)


def _attn_inproj(x, g, wq, wk, wv, gq_row, gk_row, gmat):
    T = x.shape[0]
    tm = TOKEN_TILE
    row = pl.BlockSpec((tm, D_MODEL), lambda i: (i, 0))
    out = jax.ShapeDtypeStruct((T, D_MODEL), BF16)
    return pl.pallas_call(
        _attn_inproj_kernel,
        grid=(T // tm,),
        in_specs=[row, _resident((1, D_MODEL)),
                  _resident((D_MODEL, D_MODEL)), _resident((D_MODEL, D_MODEL)),
                  _resident((D_MODEL, D_MODEL)),
                  _resident((1, D_MODEL)), _resident((1, D_MODEL)),
                  _resident((NORM_GROUP_TILE, NORM_GROUP_TILE))],
        out_specs=[row, row, row],
        out_shape=[out, out, out],
        compiler_params=pltpu.CompilerParams(
            dimension_semantics=("parallel",), vmem_limit_bytes=VMEM_LIMIT),
        name="attn_inproj",
    )(x, g, wq, wk, wv, gq_row, gk_row, gmat)


def _attn_core_kernel(lambda_init, slopes_ref, q_ref, k_ref, v_ref, bias_ref,
                      lq1_ref, lk1_ref, lq2_ref, lk2_ref, gsub_ref, o_ref,
                      qcat_ref, m_ref, acc_ref, s0_ref):
    tq, tk = ATTN_TQ, ATTN_TK
    qi = pl.program_id(1)

    q = q_ref[0]
    lane = lax.broadcasted_iota(jnp.int32, (tq, A_VDIM), 1)
    for h in range(A_HEADS):
        qh = q[:, h * A_VDIM:(h + 1) * A_VDIM]
        zero = jnp.zeros_like(qh)
        qcat_ref[h, :tq, :] = jnp.where(lane < A_DHEAD, qh, zero)
        qcat_ref[h, tq:, :] = jnp.where(lane >= A_DHEAD, qh, zero)
    m_ref[...] = jnp.full_like(m_ref, NEG_BIG)
    acc_ref[...] = jnp.zeros_like(acc_ref)
    s0_ref[...] = _dot_nt(k_ref[0, :tk, :A_VDIM], qcat_ref[0])

    def body(kj, _):
        start = pl.multiple_of(kj * tk, tk)
        diag = jnp.where(kj == qi, 1, 0)
        dist = ((qi - kj) * tq).astype(F32)
        def scores(h, at):
            k = k_ref[0, pl.ds(at, tk), h * A_VDIM:(h + 1) * A_VDIM]
            return _dot_nt(k, qcat_ref[h])

        s_next = s0_ref[...]
        for h in range(A_HEADS):
            s = s_next
            if h + 1 < A_HEADS:
                s_next = scores(h + 1, start)
            else:
                nxt = pl.multiple_of(jnp.minimum(kj + 1, qi) * tk, tk)
                s0_ref[...] = scores(0, nxt)
            v = v_ref[0, pl.ds(start, tk), h * A_VDIM:(h + 1) * A_VDIM]
            off = -slopes_ref[h] * dist
            bias = bias_ref[diag * A_HEADS + h]
            m = m_ref[h]
            m_new, p = [], []
            for cb in range(2 * tq // LANES):
                cs = slice(cb * LANES, (cb + 1) * LANES)
                bs = slice((cb * LANES) % tq, (cb * LANES) % tq + LANES)
                sb = s[:, cs] + bias[:, bs]
                mb = jnp.maximum(m[:, cs], jnp.max(sb, axis=0, keepdims=True) + off)
                p.append(jnp.exp2(sb - (mb - off)).astype(BF16))
                m_new.append(mb)
            m_new = jnp.concatenate(m_new, axis=1)
            alpha = jnp.exp2(m - m_new)
            v_aug = jnp.concatenate([v.T, jnp.ones((ONES_ROWS, tk), BF16)], axis=0)
            acc_ref[h] = alpha * acc_ref[h] + _dot(v_aug, jnp.concatenate(p, axis=1))
            m_ref[h] = m_new
        return 0

    lax.fori_loop(0, qi + 1, body, 0)

    lam = (jnp.exp(jnp.sum(lq1_ref[...] * lk1_ref[...], keepdims=True))
           - jnp.exp(jnp.sum(lq2_ref[...] * lk2_ref[...], keepdims=True))
           + lambda_init)
    for h in range(A_HEADS):
        acc = acc_ref[h]
        a = acc[:A_VDIM] / acc[A_VDIM:A_VDIM + 1]
        o = a[:, :tq] - lam * a[:, tq:]
        o = o * lax.rsqrt(jnp.mean(o * o, axis=0, keepdims=True) + EPS) * gsub_ref[...]
        o_ref[0, :, h * A_VDIM:(h + 1) * A_VDIM] = (o * (1.0 - lambda_init)).T.astype(BF16)


def _attn_core(q, k, v, slopes, bias, lq1, lk1, lq2, lk2, gsub_col, lambda_init):
    B, S, W = q.shape
    tq, tk = ATTN_TQ, ATTN_TK
    qspec = pl.BlockSpec((1, tq, W), lambda b, i: (b, i, 0))
    kvspec = pl.BlockSpec((1, S, W), lambda b, i: (b, 0, 0))
    vec = lambda n: pl.BlockSpec((1, n), lambda b, i: (0, 0))
    return pl.pallas_call(
        functools.partial(_attn_core_kernel, lambda_init),
        grid=(B, S // tq),
        in_specs=[pl.BlockSpec(memory_space=pltpu.SMEM),
                  qspec, kvspec, kvspec, _resident(bias.shape),
                  vec(A_DHEAD), vec(A_DHEAD), vec(A_DHEAD), vec(A_DHEAD),
                  pl.BlockSpec((A_VDIM, 1), lambda b, i: (0, 0))],
        out_specs=qspec,
        out_shape=jax.ShapeDtypeStruct((B, S, W), BF16),
        scratch_shapes=[pltpu.VMEM((A_HEADS, 2 * tq, A_VDIM), BF16),
                        pltpu.VMEM((A_HEADS, 1, 2 * tq), F32),
                        pltpu.VMEM((A_HEADS, A_VDIM + ONES_ROWS, 2 * tq), F32),
                        pltpu.VMEM((tk, 2 * tq), F32)],
        compiler_params=pltpu.CompilerParams(
            dimension_semantics=("parallel", "parallel"), vmem_limit_bytes=VMEM_LIMIT),
        name="attn_core",
    )(slopes, q, k, v, bias, lq1, lk1, lq2, lk2, gsub_col)


def _ret_inproj_kernel(x_ref, g_ref, w_ref, o_ref):
    xn = _rms(x_ref[...], g_ref[...]).astype(BF16)
    n_out = o_ref.shape[1]
    for c in range(n_out // D_MODEL):
        sl = slice(c * D_MODEL, (c + 1) * D_MODEL)
        o_ref[:, sl] = _dot(xn, w_ref[:, sl]).astype(BF16)


def _ret_inproj(x, g, w):
    T = x.shape[0]
    n_out = w.shape[1]
    tm = TOKEN_TILE
    return pl.pallas_call(
        _ret_inproj_kernel,
        grid=(T // tm,),
        in_specs=[pl.BlockSpec((tm, D_MODEL), lambda i: (i, 0)),
                  _resident((1, D_MODEL)), _resident((D_MODEL, n_out))],
        out_specs=pl.BlockSpec((tm, n_out), lambda i: (i, 0)),
        out_shape=jax.ShapeDtypeStruct((T, n_out), BF16),
        compiler_params=pltpu.CompilerParams(
            dimension_semantics=("parallel",), vmem_limit_bytes=VMEM_LIMIT),
        name="ret_inproj",
    )(x, g, w)


def _ret_core_kernel(cdec_ref, proj_ref, dec_ref, qdec_ref, kdec_ref, ggn_ref, o_ref, state_ref):
    @pl.when(pl.program_id(1) == 0)
    def _():
        state_ref[...] = jnp.zeros_like(state_ref)

    k0 = R_HEADS * R_DK
    v0 = 2 * R_HEADS * R_DK
    g0 = v0 + R_HEADS * R_DV

    def scores(h):
        q = proj_ref[0, :, h * R_DK:(h + 1) * R_DK]
        k = proj_ref[0, :, k0 + h * R_DK:k0 + (h + 1) * R_DK]
        return _dot_nt(q, k)

    sc_next = scores(0)
    for h in range(R_HEADS):
        sc = sc_next
        if h + 1 < R_HEADS:
            sc_next = scores(h + 1)
        q = proj_ref[0, :, h * R_DK:(h + 1) * R_DK]
        k = proj_ref[0, :, k0 + h * R_DK:k0 + (h + 1) * R_DK]
        v = proj_ref[0, :, v0 + h * R_DV:v0 + (h + 1) * R_DV]
        state = state_ref[h]
        sc = sc * dec_ref[h]
        o = _dot(sc.astype(BF16), v) + _dot(q, state.astype(BF16)) * qdec_ref[h]
        kd = (k.astype(F32) * kdec_ref[h]).astype(BF16)
        state_ref[h] = cdec_ref[h] * state + _dot_tn(kd, v)
        gate = proj_ref[0, :, g0 + h * R_DV:g0 + (h + 1) * R_DV].astype(F32)
        o_ref[0, :, h * R_DV:(h + 1) * R_DV] = (
            gate * jax.nn.sigmoid(gate) * _rms(o, ggn_ref[...])).astype(BF16)


def _ret_core(proj, cdec, dec, qdec, kdec, ggn):
    B, S, W = proj.shape
    C = RET_CHUNK
    return pl.pallas_call(
        _ret_core_kernel,
        grid=(B, S // C),
        in_specs=[pl.BlockSpec(memory_space=pltpu.SMEM),
                  pl.BlockSpec((1, C, W), lambda b, n: (b, n, 0)),
                  _resident(dec.shape), _resident(qdec.shape), _resident(kdec.shape),
                  _resident((1, R_DV))],
        out_specs=pl.BlockSpec((1, C, R_HEADS * R_DV), lambda b, n: (b, n, 0)),
        out_shape=jax.ShapeDtypeStruct((B, S, R_HEADS * R_DV), BF16),
        scratch_shapes=[pltpu.VMEM((R_HEADS, R_DK, R_DV), F32)],
        compiler_params=pltpu.CompilerParams(
            dimension_semantics=("parallel", "arbitrary"), vmem_limit_bytes=VMEM_LIMIT),
        name="ret_core",
    )(cdec, proj, dec, qdec, kdec, ggn)


def _tail_kernel(x_ref, y_ref, p_ref, wo_ref, gmlp_ref, w1_ref, w2_ref, gpe_ref, wg_ref,
                 wup_ref, o_ref):
    x = x_ref[...] + _dot(y_ref[...], wo_ref[...])
    xn = _rms(x, gmlp_ref[...]).astype(BF16)
    acc = x
    for c in range(D_FF // FF_CHUNK):
        sl = slice(c * FF_CHUNK, (c + 1) * FF_CHUNK)
        hid = jnp.maximum(_dot(xn, w1_ref[:, sl]), 0.0)
        acc = acc + _dot((hid * hid).astype(BF16), w2_ref[sl, :])
    x = acc
    gate = jax.nn.sigmoid(_dot(_rms(x, gpe_ref[...]).astype(BF16), wg_ref[...]))
    o_ref[...] = x + gate * _dot(p_ref[...].astype(BF16), wup_ref[...])


def _tail(x, y, p, wo, gmlp, w1, w2, gpe, wg, wup):
    T = x.shape[0]
    dy = y.shape[1]
    tm = TOKEN_TILE
    row = pl.BlockSpec((tm, D_MODEL), lambda i: (i, 0))
    return pl.pallas_call(
        _tail_kernel,
        grid=(T // tm,),
        in_specs=[row,
                  pl.BlockSpec((tm, dy), lambda i: (i, 0)),
                  pl.BlockSpec((tm, PLE_DIM), lambda i: (i, 0)),
                  _resident((dy, D_MODEL)), _resident((1, D_MODEL)),
                  _resident((D_MODEL, D_FF)), _resident((D_FF, D_MODEL)),
                  _resident((1, D_MODEL)), _resident((D_MODEL, D_MODEL)),
                  _resident((PLE_DIM, D_MODEL))],
        out_specs=row,
        out_shape=jax.ShapeDtypeStruct((T, D_MODEL), F32),
        compiler_params=pltpu.CompilerParams(
            dimension_semantics=("parallel",), vmem_limit_bytes=VMEM_LIMIT),
        name="layer_tail",
    )(x, y, p, wo, gmlp, w1, w2, gpe, wg, wup)


def _attn_tables():
    slopes = jnp.array([2.0 ** (-8.0 * (h + 1) / A_HEADS) for h in range(A_HEADS)], F32)
    kpos = jnp.arange(ATTN_TK, dtype=F32)[:, None]
    qpos = jnp.arange(ATTN_TQ, dtype=F32)[None, :]
    slopes = slopes * LOG2E
    bias = -slopes[:, None, None] * (qpos - kpos)[None]
    bias = jnp.concatenate([bias, jnp.where((qpos >= kpos)[None], bias, NEG_BIG)], axis=0)
    gmat = jnp.where(jnp.arange(NORM_GROUP_TILE)[:, None] // A_DHEAD
                     == jnp.arange(NORM_GROUP_TILE)[None, :] // A_DHEAD,
                     1.0 / A_DHEAD, 0.0).astype(BF16)
    return slopes, bias, gmat


def _ret_tables():
    C = RET_CHUNK
    scale = R_DK ** -0.5
    gamma = 1.0 - 2.0 ** (-5.0 - jnp.arange(R_HEADS, dtype=F32))
    log_g = jnp.log(gamma)
    pos = jnp.arange(C, dtype=F32)
    diff = pos[:, None] - pos[None, :]
    dec = jnp.where(diff[None] >= 0,
                    jnp.exp(jnp.maximum(diff, 0.0)[None] * log_g[:, None, None]), 0.0) * scale
    qdec = jnp.exp((pos[None, :] + 1.0) * log_g[:, None])[..., None]
    kdec = (jnp.exp((C - 1.0 - pos[None, :]) * log_g[:, None]) * scale)[..., None]
    cdec = jnp.exp(C * log_g)
    return cdec, dec, qdec, kdec


def kernel(x, p, norm_mix, norm_mlp, norm_pe, a_w_qkv, a_w_o, a_g_q, a_g_k, a_lam_q1, a_lam_k1,
           a_lam_q2, a_lam_k2, a_g_sub, r_w_in, r_w_out, r_g_gn, mlp_w1, mlp_w2, pe_w_up,
           pe_w_gate):
    B, S, _ = x.shape
    T = B * S
    slopes, bias, gmat = _attn_tables()
    cdec, dec, qdec, kdec = _ret_tables()
    row = lambda a: a.reshape(1, -1).astype(F32)

    xt = x.reshape(T, D_MODEL)
    for i in range(DEPTH):
        j = i // N_MIXERS
        if i % N_MIXERS == 0:
            lambda_init = 0.8 - 0.6 * math.exp(-0.3 * i)
            w = a_w_qkv[j].astype(BF16)
            gq_row = row(jnp.tile(a_g_q[j], D_MODEL // A_DHEAD) * (A_DHEAD ** -0.5 * LOG2E))
            gk_row = row(jnp.tile(a_g_k[j], D_MODEL // A_DHEAD))
            q, k, v = _attn_inproj(xt, row(norm_mix[i]), w[:, :D_MODEL],
                                   w[:, D_MODEL:2 * D_MODEL], w[:, 2 * D_MODEL:],
                                   gq_row, gk_row, gmat)
            shp = (B, S, D_MODEL)
            mix = _attn_core(q.reshape(shp), k.reshape(shp), v.reshape(shp), slopes, bias,
                             row(a_lam_q1[j]), row(a_lam_k1[j]), row(a_lam_q2[j]),
                             row(a_lam_k2[j]), a_g_sub[j].reshape(-1, 1).astype(F32),
                             lambda_init)
            w_out = a_w_o[j]
        else:
            proj = _ret_inproj(xt, row(norm_mix[i]), r_w_in[j].astype(BF16))
            mix = _ret_core(proj.reshape(B, S, -1), cdec, dec, qdec, kdec, row(r_g_gn[j]))
            w_out = r_w_out[j]
        xt = _tail(xt, mix.reshape(T, -1), p[i].reshape(T, PLE_DIM), w_out.astype(BF16),
                   row(norm_mlp[i]), mlp_w1[i].astype(BF16), mlp_w2[i].astype(BF16),
                   row(norm_pe[i]), pe_w_gate[i].astype(BF16), pe_w_up[i].astype(BF16))
    return xt.reshape(B, S, D_MODEL)
```

```python
import functools
import math

import jax
import jax.numpy as jnp
from jax import lax
from jax.experimental import pallas as pl
from jax.experimental.pallas import tpu as pltpu

F32 = jnp.float32
BF16 = jnp.bfloat16

EPS = 1e-6
D_MODEL = 1024
DEPTH = 4
N_MIXERS = 2
A_HEADS = 8
A_DHEAD = 64
A_VDIM = 128
R_HEADS = 4
R_DK = 256
R_DV = 512
D_FF = 4096
PLE_DIM = 256

V7X_VMEM_BYTES = 64 * 1024 * 1024
VMEM_LIMIT = V7X_VMEM_BYTES - 8 * 1024 * 1024

TOKEN_TILE = 512
ATTN_TQ = 256
ATTN_TK = 256
RET_CHUNK = 256
FF_CHUNK = 1024
NORM_GROUP_TILE = 256
NEG_BIG = -1e30
LANES = 128
ONES_ROWS = 16
LOG2E = math.log2(math.e)
SCORE_LOOKAHEAD = 2


def _resident(shape):
    nd = len(shape)
    return pl.BlockSpec(shape, lambda *_: (0,) * nd, pipeline_mode=pl.Buffered(1))


def _rms(x, g):
    return x * lax.rsqrt(jnp.mean(x * x, axis=-1, keepdims=True) + EPS) * g


def _dot(a, b):
    return jnp.dot(a, b, preferred_element_type=F32)


def _dot_nt(a, b):
    return lax.dot_general(a, b, (((1,), (1,)), ((), ())), preferred_element_type=F32)


def _dot_tn(a, b):
    return lax.dot_general(a, b, (((0,), (0,)), ((), ())), preferred_element_type=F32)


def _attn_inproj_kernel(x_ref, g_ref, wq_ref, wk_ref, wv_ref, gq_ref, gk_ref, gm_ref,
                        q_ref, k_ref, v_ref):
    xn = _rms(x_ref[...], g_ref[...]).astype(BF16)

    def qk_norm(w_ref, gain_ref, o_ref):
        y = _dot(xn, w_ref[...])
        for s in range(D_MODEL // NORM_GROUP_TILE):
            sl = slice(s * NORM_GROUP_TILE, (s + 1) * NORM_GROUP_TILE)
            ys = y[:, sl]
            ms = _dot((ys * ys).astype(BF16), gm_ref[...])
            o_ref[:, sl] = (ys * lax.rsqrt(ms + EPS) * gain_ref[:, sl]).astype(BF16)

    qk_norm(wq_ref, gq_ref, q_ref)
    qk_norm(wk_ref, gk_ref, k_ref)
    v_ref[...] = _dot(xn, wv_ref[...]).astype(BF16)


def _attn_inproj(x, g, wq, wk, wv, gq_row, gk_row, gmat):
    T = x.shape[0]
    tm = TOKEN_TILE
    row = pl.BlockSpec((tm, D_MODEL), lambda i: (i, 0))
    out = jax.ShapeDtypeStruct((T, D_MODEL), BF16)
    return pl.pallas_call(
        _attn_inproj_kernel,
        grid=(T // tm,),
        in_specs=[row, _resident((1, D_MODEL)),
                  _resident((D_MODEL, D_MODEL)), _resident((D_MODEL, D_MODEL)),
                  _resident((D_MODEL, D_MODEL)),
                  _resident((1, D_MODEL)), _resident((1, D_MODEL)),
                  _resident((NORM_GROUP_TILE, NORM_GROUP_TILE))],
        out_specs=[row, row, row],
        out_shape=[out, out, out],
        compiler_params=pltpu.CompilerParams(
            dimension_semantics=("parallel",), vmem_limit_bytes=VMEM_LIMIT),
        name="attn_inproj",
    )(x, g, wq, wk, wv, gq_row, gk_row, gmat)


def _attn_core_kernel(lambda_init, slopes_ref, q_ref, k_ref, v_ref, bias_ref,
                      lq1_ref, lk1_ref, lq2_ref, lk2_ref, gsub_ref, o_ref,
                      qcat_ref, m_ref, acc_ref, s0_ref):
    tq, tk = ATTN_TQ, ATTN_TK
    qi = pl.program_id(1)

    q = q_ref[0]
    lane = lax.broadcasted_iota(jnp.int32, (tq, A_VDIM), 1)
    for h in range(A_HEADS):
        qh = q[:, h * A_VDIM:(h + 1) * A_VDIM]
        zero = jnp.zeros_like(qh)
        qcat_ref[h, :tq, :] = jnp.where(lane < A_DHEAD, qh, zero)
        qcat_ref[h, tq:, :] = jnp.where(lane >= A_DHEAD, qh, zero)
    m_ref[...] = jnp.full_like(m_ref, NEG_BIG)
    acc_ref[...] = jnp.zeros_like(acc_ref)

    def scores(h, kj):
        k = k_ref[0, pl.ds(pl.multiple_of(kj * tk, tk), tk), h * A_VDIM:(h + 1) * A_VDIM]
        s = _dot_nt(k, qcat_ref[h])
        bias = bias_ref[jnp.where(kj == qi, A_HEADS, 0) + h]
        return jnp.concatenate([s[:, :tq] + bias, s[:, tq:] + bias], axis=1).astype(BF16)

    for a in range(SCORE_LOOKAHEAD):
        s0_ref[a] = scores(a, 0)

    def body(kj, _):
        start = pl.multiple_of(kj * tk, tk)
        dist = ((qi - kj) * tq).astype(F32)
        queue = [s0_ref[a] for a in range(SCORE_LOOKAHEAD)]
        for h in range(A_HEADS):
            s = queue.pop(0)
            ahead = h + SCORE_LOOKAHEAD
            if ahead < A_HEADS:
                queue.append(scores(ahead, kj))
            else:
                s0_ref[ahead - A_HEADS] = scores(ahead - A_HEADS, jnp.minimum(kj + 1, qi))
            v = v_ref[0, pl.ds(start, tk), h * A_VDIM:(h + 1) * A_VDIM]
            off = -slopes_ref[h] * dist
            m = m_ref[h]
            m_new, p = [], []
            for cb in range(2 * tq // LANES):
                cs = slice(cb * LANES, (cb + 1) * LANES)
                sb = s[:, cs]
                cmax = jnp.max(sb, axis=0, keepdims=True).astype(F32)
                shift = (jnp.maximum(m[:, cs], cmax + off) - off).astype(BF16)
                p.append(jnp.exp2(sb - shift))
                m_new.append(shift.astype(F32) + off)
            m_new = jnp.concatenate(m_new, axis=1)
            alpha = jnp.exp2(m - m_new)
            v_aug = jnp.concatenate([v.T, jnp.ones((ONES_ROWS, tk), BF16)], axis=0)
            acc_ref[h] = alpha * acc_ref[h] + _dot(v_aug, jnp.concatenate(p, axis=1))
            m_ref[h] = m_new
        return 0

    lax.fori_loop(0, qi + 1, body, 0)

    lam = (jnp.exp(jnp.sum(lq1_ref[...] * lk1_ref[...], keepdims=True))
           - jnp.exp(jnp.sum(lq2_ref[...] * lk2_ref[...], keepdims=True))
           + lambda_init)
    for h in range(A_HEADS):
        acc = acc_ref[h]
        a = acc[:A_VDIM] / acc[A_VDIM:A_VDIM + 1]
        o = a[:, :tq] - lam * a[:, tq:]
        o = o * lax.rsqrt(jnp.mean(o * o, axis=0, keepdims=True) + EPS) * gsub_ref[...]
        o_ref[0, :, h * A_VDIM:(h + 1) * A_VDIM] = (o * (1.0 - lambda_init)).T.astype(BF16)


def _attn_core(q, k, v, slopes, bias, lq1, lk1, lq2, lk2, gsub_col, lambda_init):
    B, S, W = q.shape
    tq, tk = ATTN_TQ, ATTN_TK
    qspec = pl.BlockSpec((1, tq, W), lambda b, i: (b, i, 0))
    kvspec = pl.BlockSpec((1, S, W), lambda b, i: (b, 0, 0))
    vec = lambda n: pl.BlockSpec((1, n), lambda b, i: (0, 0))
    return pl.pallas_call(
        functools.partial(_attn_core_kernel, lambda_init),
        grid=(B, S // tq),
        in_specs=[pl.BlockSpec(memory_space=pltpu.SMEM),
                  qspec, kvspec, kvspec, _resident(bias.shape),
                  vec(A_DHEAD), vec(A_DHEAD), vec(A_DHEAD), vec(A_DHEAD),
                  pl.BlockSpec((A_VDIM, 1), lambda b, i: (0, 0))],
        out_specs=qspec,
        out_shape=jax.ShapeDtypeStruct((B, S, W), BF16),
        scratch_shapes=[pltpu.VMEM((A_HEADS, 2 * tq, A_VDIM), BF16),
                        pltpu.VMEM((A_HEADS, 1, 2 * tq), F32),
                        pltpu.VMEM((A_HEADS, A_VDIM + ONES_ROWS, 2 * tq), F32),
                        pltpu.VMEM((SCORE_LOOKAHEAD, tk, 2 * tq), BF16)],
        compiler_params=pltpu.CompilerParams(
            dimension_semantics=("parallel", "parallel"), vmem_limit_bytes=VMEM_LIMIT),
        name="attn_core",
    )(slopes, q, k, v, bias, lq1, lk1, lq2, lk2, gsub_col)


def _ret_inproj_kernel(x_ref, g_ref, w_ref, o_ref):
    xn = _rms(x_ref[...], g_ref[...]).astype(BF16)
    n_out = o_ref.shape[1]
    for c in range(n_out // D_MODEL):
        sl = slice(c * D_MODEL, (c + 1) * D_MODEL)
        o_ref[:, sl] = _dot(xn, w_ref[:, sl]).astype(BF16)


def _ret_inproj(x, g, w):
    T = x.shape[0]
    n_out = w.shape[1]
    tm = TOKEN_TILE
    return pl.pallas_call(
        _ret_inproj_kernel,
        grid=(T // tm,),
        in_specs=[pl.BlockSpec((tm, D_MODEL), lambda i: (i, 0)),
                  _resident((1, D_MODEL)), _resident((D_MODEL, n_out))],
        out_specs=pl.BlockSpec((tm, n_out), lambda i: (i, 0)),
        out_shape=jax.ShapeDtypeStruct((T, n_out), BF16),
        compiler_params=pltpu.CompilerParams(
            dimension_semantics=("parallel",), vmem_limit_bytes=VMEM_LIMIT),
        name="ret_inproj",
    )(x, g, w)


def _ret_core_kernel(cdec_ref, proj_ref, dec_ref, qdec_ref, kdec_ref, ggn_ref, o_ref, state_ref):
    @pl.when(pl.program_id(1) == 0)
    def _():
        state_ref[...] = jnp.zeros_like(state_ref)

    k0 = R_HEADS * R_DK
    v0 = 2 * R_HEADS * R_DK
    g0 = v0 + R_HEADS * R_DV

    def scores(h):
        q = proj_ref[0, :, h * R_DK:(h + 1) * R_DK]
        k = proj_ref[0, :, k0 + h * R_DK:k0 + (h + 1) * R_DK]
        return _dot_nt(q, k)

    sc_next = scores(0)
    for h in range(R_HEADS):
        sc = sc_next
        if h + 1 < R_HEADS:
            sc_next = scores(h + 1)
        q = proj_ref[0, :, h * R_DK:(h + 1) * R_DK]
        k = proj_ref[0, :, k0 + h * R_DK:k0 + (h + 1) * R_DK]
        v = proj_ref[0, :, v0 + h * R_DV:v0 + (h + 1) * R_DV]
        state = state_ref[h]
        sc = sc * dec_ref[h]
        o = _dot(sc.astype(BF16), v) + _dot(q, state.astype(BF16)) * qdec_ref[h]
        kd = (k.astype(F32) * kdec_ref[h]).astype(BF16)
        state_ref[h] = cdec_ref[h] * state + _dot_tn(kd, v)
        gate = proj_ref[0, :, g0 + h * R_DV:g0 + (h + 1) * R_DV].astype(F32)
        o_ref[0, :, h * R_DV:(h + 1) * R_DV] = (
            gate * jax.nn.sigmoid(gate) * _rms(o, ggn_ref[...])).astype(BF16)


def _ret_core(proj, cdec, dec, qdec, kdec, ggn):
    B, S, W = proj.shape
    C = RET_CHUNK
    return pl.pallas_call(
        _ret_core_kernel,
        grid=(B, S // C),
        in_specs=[pl.BlockSpec(memory_space=pltpu.SMEM),
                  pl.BlockSpec((1, C, W), lambda b, n: (b, n, 0)),
                  _resident(dec.shape), _resident(qdec.shape), _resident(kdec.shape),
                  _resident((1, R_DV))],
        out_specs=pl.BlockSpec((1, C, R_HEADS * R_DV), lambda b, n: (b, n, 0)),
        out_shape=jax.ShapeDtypeStruct((B, S, R_HEADS * R_DV), BF16),
        scratch_shapes=[pltpu.VMEM((R_HEADS, R_DK, R_DV), F32)],
        compiler_params=pltpu.CompilerParams(
            dimension_semantics=("parallel", "arbitrary"), vmem_limit_bytes=VMEM_LIMIT),
        name="ret_core",
    )(cdec, proj, dec, qdec, kdec, ggn)


def _tail_kernel(x_ref, y_ref, p_ref, wo_ref, gmlp_ref, w1_ref, w2_ref, gpe_ref, wg_ref,
                 wup_ref, o_ref):
    x = x_ref[...] + _dot(y_ref[...], wo_ref[...])
    xn = _rms(x, gmlp_ref[...]).astype(BF16)
    acc = x
    for c in range(D_FF // FF_CHUNK):
        sl = slice(c * FF_CHUNK, (c + 1) * FF_CHUNK)
        hid = jnp.maximum(_dot(xn, w1_ref[:, sl]), 0.0)
        acc = acc + _dot((hid * hid).astype(BF16), w2_ref[sl, :])
    x = acc
    gate = jax.nn.sigmoid(_dot(_rms(x, gpe_ref[...]).astype(BF16), wg_ref[...]))
    o_ref[...] = x + gate * _dot(p_ref[...].astype(BF16), wup_ref[...])


def _tail(x, y, p, wo, gmlp, w1, w2, gpe, wg, wup):
    T = x.shape[0]
    dy = y.shape[1]
    tm = TOKEN_TILE
    row = pl.BlockSpec((tm, D_MODEL), lambda i: (i, 0))
    return pl.pallas_call(
        _tail_kernel,
        grid=(T // tm,),
        in_specs=[row,
                  pl.BlockSpec((tm, dy), lambda i: (i, 0)),
                  pl.BlockSpec((tm, PLE_DIM), lambda i: (i, 0)),
                  _resident((dy, D_MODEL)), _resident((1, D_MODEL)),
                  _resident((D_MODEL, D_FF)), _resident((D_FF, D_MODEL)),
                  _resident((1, D_MODEL)), _resident((D_MODEL, D_MODEL)),
                  _resident((PLE_DIM, D_MODEL))],
        out_specs=row,
        out_shape=jax.ShapeDtypeStruct((T, D_MODEL), F32),
        compiler_params=pltpu.CompilerParams(
            dimension_semantics=("parallel",), vmem_limit_bytes=VMEM_LIMIT),
        name="layer_tail",
    )(x, y, p, wo, gmlp, w1, w2, gpe, wg, wup)


def _attn_tables():
    slopes = jnp.array([2.0 ** (-8.0 * (h + 1) / A_HEADS) for h in range(A_HEADS)], F32)
    kpos = jnp.arange(ATTN_TK, dtype=F32)[:, None]
    qpos = jnp.arange(ATTN_TQ, dtype=F32)[None, :]
    slopes = slopes * LOG2E
    bias = -slopes[:, None, None] * (qpos - kpos)[None]
    bias = jnp.concatenate([bias, jnp.where((qpos >= kpos)[None], bias, NEG_BIG)], axis=0)
    gmat = jnp.where(jnp.arange(NORM_GROUP_TILE)[:, None] // A_DHEAD
                     == jnp.arange(NORM_GROUP_TILE)[None, :] // A_DHEAD,
                     1.0 / A_DHEAD, 0.0).astype(BF16)
    return slopes, bias, gmat


def _ret_tables():
    C = RET_CHUNK
    scale = R_DK ** -0.5
    gamma = 1.0 - 2.0 ** (-5.0 - jnp.arange(R_HEADS, dtype=F32))
    log_g = jnp.log(gamma)
    pos = jnp.arange(C, dtype=F32)
    diff = pos[:, None] - pos[None, :]
    dec = jnp.where(diff[None] >= 0,
                    jnp.exp(jnp.maximum(diff, 0.0)[None] * log_g[:, None, None]), 0.0) * scale
    qdec = jnp.exp((pos[None, :] + 1.0) * log_g[:, None])[..., None]
    kdec = (jnp.exp((C - 1.0 - pos[None, :]) * log_g[:, None]) * scale)[..., None]
    cdec = jnp.exp(C * log_g)
    return cdec, dec, qdec, kdec


def kernel(x, p, norm_mix, norm_mlp, norm_pe, a_w_qkv, a_w_o, a_g_q, a_g_k, a_lam_q1, a_lam_k1,
           a_lam_q2, a_lam_k2, a_g_sub, r_w_in, r_w_out, r_g_gn, mlp_w1, mlp_w2, pe_w_up,
           pe_w_gate):
    B, S, _ = x.shape
    T = B * S
    slopes, bias, gmat = _attn_tables()
    cdec, dec, qdec, kdec = _ret_tables()
    row = lambda a: a.reshape(1, -1).astype(F32)

    xt = x.reshape(T, D_MODEL)
    for i in range(DEPTH):
        j = i // N_MIXERS
        if i % N_MIXERS == 0:
            lambda_init = 0.8 - 0.6 * math.exp(-0.3 * i)
            w = a_w_qkv[j].astype(BF16)
            gq_row = row(jnp.tile(a_g_q[j], D_MODEL // A_DHEAD) * (A_DHEAD ** -0.5 * LOG2E))
            gk_row = row(jnp.tile(a_g_k[j], D_MODEL // A_DHEAD))
            q, k, v = _attn_inproj(xt, row(norm_mix[i]), w[:, :D_MODEL],
                                   w[:, D_MODEL:2 * D_MODEL], w[:, 2 * D_MODEL:],
                                   gq_row, gk_row, gmat)
            shp = (B, S, D_MODEL)
            mix = _attn_core(q.reshape(shp), k.reshape(shp), v.reshape(shp), slopes, bias,
                             row(a_lam_q1[j]), row(a_lam_k1[j]), row(a_lam_q2[j]),
                             row(a_lam_k2[j]), a_g_sub[j].reshape(-1, 1).astype(F32),
                             lambda_init)
            w_out = a_w_o[j]
        else:
            proj = _ret_inproj(xt, row(norm_mix[i]), r_w_in[j].astype(BF16))
            mix = _ret_core(proj.reshape(B, S, -1), cdec, dec, qdec, kdec, row(r_g_gn[j]))
            w_out = r_w_out[j]
        xt = _tail(xt, mix.reshape(T, -1), p[i].reshape(T, PLE_DIM), w_out.astype(BF16),
                   row(norm_mlp[i]), mlp_w1[i].astype(BF16), mlp_w2[i].astype(BF16),
                   row(norm_pe[i]), pe_w_gate[i].astype(BF16), pe_w_up[i].astype(BF16))
    return xt.reshape(B, S, D_MODEL)
```

```python
import functools
import math

import jax
import jax.numpy as jnp
from jax import lax
from jax.experimental import pallas as pl
from jax.experimental.pallas import tpu as pltpu

F32 = jnp.float32
BF16 = jnp.bfloat16

EPS = 1e-6
D_MODEL = 1024
DEPTH = 4
N_MIXERS = 2
A_HEADS = 8
A_DHEAD = 64
A_VDIM = 128
R_HEADS = 4
R_DK = 256
R_DV = 512
D_FF = 4096
PLE_DIM = 256

V7X_VMEM_BYTES = 64 * 1024 * 1024
VMEM_LIMIT = V7X_VMEM_BYTES - 8 * 1024 * 1024

TOKEN_TILE = 512
ATTN_TQ = 256
ATTN_TK = 256
RET_CHUNK = 256
FF_CHUNK = 1024
NORM_GROUP_TILE = 256
NEG_BIG = -1e30
LANES = 128
ONES_ROWS = 16
LOG2E = math.log2(math.e)
SCORE_LOOKAHEAD = 2


def _resident(shape):
    nd = len(shape)
    return pl.BlockSpec(shape, lambda *_: (0,) * nd, pipeline_mode=pl.Buffered(1))


def _rms(x, g):
    return x * lax.rsqrt(jnp.mean(x * x, axis=-1, keepdims=True) + EPS) * g


def _dot(a, b):
    return jnp.dot(a, b, preferred_element_type=F32)


def _dot_nt(a, b):
    return lax.dot_general(a, b, (((1,), (1,)), ((), ())), preferred_element_type=F32)


def _dot_tn(a, b):
    return lax.dot_general(a, b, (((0,), (0,)), ((), ())), preferred_element_type=F32)


def _attn_inproj_kernel(x_ref, g_ref, wq_ref, wk_ref, wv_ref, gq_ref, gk_ref, gm_ref,
                        q_ref, k_ref, v_ref):
    xn = _rms(x_ref[...], g_ref[...]).astype(BF16)

    def qk_norm(w_ref, gain_ref, o_ref):
        y = _dot(xn, w_ref[...])
        for s in range(D_MODEL // NORM_GROUP_TILE):
            sl = slice(s * NORM_GROUP_TILE, (s + 1) * NORM_GROUP_TILE)
            ys = y[:, sl]
            ms = _dot((ys * ys).astype(BF16), gm_ref[...])
            o_ref[:, sl] = (ys * lax.rsqrt(ms + EPS) * gain_ref[:, sl]).astype(BF16)

    qk_norm(wq_ref, gq_ref, q_ref)
    qk_norm(wk_ref, gk_ref, k_ref)
    v_ref[...] = _dot(xn, wv_ref[...]).astype(BF16)


def _attn_inproj(x, g, w_qkv, gq_row, gk_row, gmat):
    T = x.shape[0]
    tm = TOKEN_TILE
    row = pl.BlockSpec((tm, D_MODEL), lambda i: (i, 0))
    out = jax.ShapeDtypeStruct((T, D_MODEL), BF16)
    wcol = lambda c: pl.BlockSpec((D_MODEL, D_MODEL), lambda i: (0, c),
                                  pipeline_mode=pl.Buffered(1))
    return pl.pallas_call(
        _attn_inproj_kernel,
        grid=(T // tm,),
        in_specs=[row, _resident((1, D_MODEL)), wcol(0), wcol(1), wcol(2),
                  _resident((1, D_MODEL)), _resident((1, D_MODEL)),
                  _resident((NORM_GROUP_TILE, NORM_GROUP_TILE))],
        out_specs=[row, row, row],
        out_shape=[out, out, out],
        compiler_params=pltpu.CompilerParams(
            dimension_semantics=("parallel",), vmem_limit_bytes=VMEM_LIMIT),
        name="attn_inproj",
    )(x, g, w_qkv, w_qkv, w_qkv, gq_row, gk_row, gmat)


def _attn_core_kernel(lambda_init, slopes_ref, q_ref, k_ref, v_ref, bias_ref,
                      lq1_ref, lk1_ref, lq2_ref, lk2_ref, gsub_ref, o_ref,
                      qcat_ref, m_ref, acc_ref, s0_ref):
    tq, tk = ATTN_TQ, ATTN_TK
    qi = pl.program_id(1)

    q = q_ref[0]
    lane = lax.broadcasted_iota(jnp.int32, (tq, A_VDIM), 1)
    for h in range(A_HEADS):
        qh = q[:, h * A_VDIM:(h + 1) * A_VDIM]
        zero = jnp.zeros_like(qh)
        qcat_ref[h, :tq, :] = jnp.where(lane < A_DHEAD, qh, zero)
        qcat_ref[h, tq:, :] = jnp.where(lane >= A_DHEAD, qh, zero)
    m_ref[...] = jnp.full_like(m_ref, NEG_BIG)
    acc_ref[...] = jnp.zeros_like(acc_ref)

    def scores(h, kj):
        k = k_ref[0, pl.ds(pl.multiple_of(kj * tk, tk), tk), h * A_VDIM:(h + 1) * A_VDIM]
        s = _dot_nt(k, qcat_ref[h])
        bias = bias_ref[jnp.where(kj == qi, A_HEADS, 0) + h]
        return jnp.concatenate([s[:, :tq] + bias, s[:, tq:] + bias], axis=1).astype(BF16)

    for a in range(SCORE_LOOKAHEAD):
        s0_ref[a] = scores(a, 0)

    def body(kj, _):
        start = pl.multiple_of(kj * tk, tk)
        dist = ((qi - kj) * tq).astype(F32)
        queue = [s0_ref[a] for a in range(SCORE_LOOKAHEAD)]
        for h in range(A_HEADS):
            s = queue.pop(0)
            ahead = h + SCORE_LOOKAHEAD
            if ahead < A_HEADS:
                queue.append(scores(ahead, kj))
            else:
                s0_ref[ahead - A_HEADS] = scores(ahead - A_HEADS, jnp.minimum(kj + 1, qi))
            v = v_ref[0, pl.ds(start, tk), h * A_VDIM:(h + 1) * A_VDIM]
            off = -slopes_ref[h] * dist
            m = m_ref[h]
            m_new, p = [], []
            for cb in range(2 * tq // LANES):
                cs = slice(cb * LANES, (cb + 1) * LANES)
                sb = s[:, cs]
                cmax = jnp.max(sb, axis=0, keepdims=True).astype(F32)
                shift = (jnp.maximum(m[:, cs], cmax + off) - off).astype(BF16)
                p.append(jnp.exp2(sb - shift))
                m_new.append(shift.astype(F32) + off)
            m_new = jnp.concatenate(m_new, axis=1)
            alpha = jnp.exp2(m - m_new)
            v_aug = jnp.concatenate([v.T, jnp.ones((ONES_ROWS, tk), BF16)], axis=0)
            acc_ref[h] = alpha * acc_ref[h] + _dot(v_aug, jnp.concatenate(p, axis=1))
            m_ref[h] = m_new
        return 0

    lax.fori_loop(0, qi + 1, body, 0)

    lam = (jnp.exp(jnp.sum(lq1_ref[...] * lk1_ref[...], keepdims=True))
           - jnp.exp(jnp.sum(lq2_ref[...] * lk2_ref[...], keepdims=True))
           + lambda_init)
    for h in range(A_HEADS):
        acc = acc_ref[h]
        a = acc[:A_VDIM] / acc[A_VDIM:A_VDIM + 1]
        o = a[:, :tq] - lam * a[:, tq:]
        o = o * lax.rsqrt(jnp.mean(o * o, axis=0, keepdims=True) + EPS) * gsub_ref[...]
        o_ref[0, :, h * A_VDIM:(h + 1) * A_VDIM] = (o * (1.0 - lambda_init)).T.astype(BF16)


def _attn_core(q, k, v, slopes, bias, lq1, lk1, lq2, lk2, gsub_col, lambda_init):
    B, S, W = q.shape
    tq, tk = ATTN_TQ, ATTN_TK
    qspec = pl.BlockSpec((1, tq, W), lambda b, i: (b, i, 0))
    kvspec = pl.BlockSpec((1, S, W), lambda b, i: (b, 0, 0))
    vec = lambda n: pl.BlockSpec((1, n), lambda b, i: (0, 0))
    return pl.pallas_call(
        functools.partial(_attn_core_kernel, lambda_init),
        grid=(B, S // tq),
        in_specs=[pl.BlockSpec(memory_space=pltpu.SMEM),
                  qspec, kvspec, kvspec, _resident(bias.shape),
                  vec(A_DHEAD), vec(A_DHEAD), vec(A_DHEAD), vec(A_DHEAD),
                  pl.BlockSpec((A_VDIM, 1), lambda b, i: (0, 0))],
        out_specs=qspec,
        out_shape=jax.ShapeDtypeStruct((B, S, W), BF16),
        scratch_shapes=[pltpu.VMEM((A_HEADS, 2 * tq, A_VDIM), BF16),
                        pltpu.VMEM((A_HEADS, 1, 2 * tq), F32),
                        pltpu.VMEM((A_HEADS, A_VDIM + ONES_ROWS, 2 * tq), F32),
                        pltpu.VMEM((SCORE_LOOKAHEAD, tk, 2 * tq), BF16)],
        compiler_params=pltpu.CompilerParams(
            dimension_semantics=("parallel", "parallel"), vmem_limit_bytes=VMEM_LIMIT),
        name="attn_core",
    )(slopes, q, k, v, bias, lq1, lk1, lq2, lk2, gsub_col)


def _ret_inproj_kernel(x_ref, g_ref, w_ref, o_ref):
    xn = _rms(x_ref[...], g_ref[...]).astype(BF16)
    n_out = o_ref.shape[1]
    for c in range(n_out // D_MODEL):
        sl = slice(c * D_MODEL, (c + 1) * D_MODEL)
        o_ref[:, sl] = _dot(xn, w_ref[:, sl]).astype(BF16)


def _ret_inproj(x, g, w):
    T = x.shape[0]
    n_out = w.shape[1]
    tm = TOKEN_TILE
    return pl.pallas_call(
        _ret_inproj_kernel,
        grid=(T // tm,),
        in_specs=[pl.BlockSpec((tm, D_MODEL), lambda i: (i, 0)),
                  _resident((1, D_MODEL)), _resident((D_MODEL, n_out))],
        out_specs=pl.BlockSpec((tm, n_out), lambda i: (i, 0)),
        out_shape=jax.ShapeDtypeStruct((T, n_out), BF16),
        compiler_params=pltpu.CompilerParams(
            dimension_semantics=("parallel",), vmem_limit_bytes=VMEM_LIMIT),
        name="ret_inproj",
    )(x, g, w)


def _ret_core_kernel(cdec_ref, proj_ref, dec_ref, qdec_ref, kdec_ref, ggn_ref, o_ref, state_ref):
    @pl.when(pl.program_id(1) == 0)
    def _():
        state_ref[...] = jnp.zeros_like(state_ref)

    k0 = R_HEADS * R_DK
    v0 = 2 * R_HEADS * R_DK
    g0 = v0 + R_HEADS * R_DV

    def scores(h):
        q = proj_ref[0, :, h * R_DK:(h + 1) * R_DK]
        k = proj_ref[0, :, k0 + h * R_DK:k0 + (h + 1) * R_DK]
        return (_dot_nt(q, k) * dec_ref[h]).astype(BF16)

    queue = [scores(a) for a in range(SCORE_LOOKAHEAD)]
    for h in range(R_HEADS):
        sc = queue.pop(0)
        if h + SCORE_LOOKAHEAD < R_HEADS:
            queue.append(scores(h + SCORE_LOOKAHEAD))
        q = proj_ref[0, :, h * R_DK:(h + 1) * R_DK]
        k = proj_ref[0, :, k0 + h * R_DK:k0 + (h + 1) * R_DK]
        v = proj_ref[0, :, v0 + h * R_DV:v0 + (h + 1) * R_DV]
        state = state_ref[h]
        o = _dot(sc, v) + _dot(q, state.astype(BF16)) * qdec_ref[h]
        kd = (k.astype(F32) * kdec_ref[h]).astype(BF16)
        state_ref[h] = cdec_ref[h] * state + _dot_tn(kd, v)
        gate = proj_ref[0, :, g0 + h * R_DV:g0 + (h + 1) * R_DV].astype(F32)
        o_ref[0, :, h * R_DV:(h + 1) * R_DV] = (
            gate * jax.nn.sigmoid(gate) * _rms(o, ggn_ref[...])).astype(BF16)


def _ret_core(proj, cdec, dec, qdec, kdec, ggn):
    B, S, W = proj.shape
    C = RET_CHUNK
    return pl.pallas_call(
        _ret_core_kernel,
        grid=(B, S // C),
        in_specs=[pl.BlockSpec(memory_space=pltpu.SMEM),
                  pl.BlockSpec((1, C, W), lambda b, n: (b, n, 0)),
                  _resident(dec.shape), _resident(qdec.shape), _resident(kdec.shape),
                  _resident((1, R_DV))],
        out_specs=pl.BlockSpec((1, C, R_HEADS * R_DV), lambda b, n: (b, n, 0)),
        out_shape=jax.ShapeDtypeStruct((B, S, R_HEADS * R_DV), BF16),
        scratch_shapes=[pltpu.VMEM((R_HEADS, R_DK, R_DV), F32)],
        compiler_params=pltpu.CompilerParams(
            dimension_semantics=("parallel", "arbitrary"), vmem_limit_bytes=VMEM_LIMIT),
        name="ret_core",
    )(cdec, proj, dec, qdec, kdec, ggn)


def _tail_kernel(x_ref, y_ref, p_ref, wo_ref, gmlp_ref, w1_ref, w2_ref, gpe_ref, wg_ref,
                 wup_ref, o_ref):
    x = x_ref[...] + _dot(y_ref[...], wo_ref[...])
    xn = _rms(x, gmlp_ref[...]).astype(BF16)
    acc = x
    for c in range(D_FF // FF_CHUNK):
        sl = slice(c * FF_CHUNK, (c + 1) * FF_CHUNK)
        hid = jnp.maximum(_dot(xn, w1_ref[:, sl]), 0.0)
        acc = acc + _dot((hid * hid).astype(BF16), w2_ref[sl, :])
    x = acc
    gate = jax.nn.sigmoid(_dot(_rms(x, gpe_ref[...]).astype(BF16), wg_ref[...]))
    o_ref[...] = x + gate * _dot(p_ref[...].astype(BF16), wup_ref[...])


def _tail(x, y, p, wo, gmlp, w1, w2, gpe, wg, wup):
    T = x.shape[0]
    dy = y.shape[1]
    tm = TOKEN_TILE
    row = pl.BlockSpec((tm, D_MODEL), lambda i: (i, 0))
    return pl.pallas_call(
        _tail_kernel,
        grid=(T // tm,),
        in_specs=[row,
                  pl.BlockSpec((tm, dy), lambda i: (i, 0)),
                  pl.BlockSpec((tm, PLE_DIM), lambda i: (i, 0)),
                  _resident((dy, D_MODEL)), _resident((1, D_MODEL)),
                  _resident((D_MODEL, D_FF)), _resident((D_FF, D_MODEL)),
                  _resident((1, D_MODEL)), _resident((D_MODEL, D_MODEL)),
                  _resident((PLE_DIM, D_MODEL))],
        out_specs=row,
        out_shape=jax.ShapeDtypeStruct((T, D_MODEL), F32),
        compiler_params=pltpu.CompilerParams(
            dimension_semantics=("parallel",), vmem_limit_bytes=VMEM_LIMIT),
        name="layer_tail",
    )(x, y, p, wo, gmlp, w1, w2, gpe, wg, wup)


def _attn_tables():
    slopes = jnp.array([2.0 ** (-8.0 * (h + 1) / A_HEADS) for h in range(A_HEADS)], F32)
    kpos = jnp.arange(ATTN_TK, dtype=F32)[:, None]
    qpos = jnp.arange(ATTN_TQ, dtype=F32)[None, :]
    slopes = slopes * LOG2E
    bias = -slopes[:, None, None] * (qpos - kpos)[None]
    bias = jnp.concatenate([bias, jnp.where((qpos >= kpos)[None], bias, NEG_BIG)], axis=0)
    gmat = jnp.where(jnp.arange(NORM_GROUP_TILE)[:, None] // A_DHEAD
                     == jnp.arange(NORM_GROUP_TILE)[None, :] // A_DHEAD,
                     1.0 / A_DHEAD, 0.0).astype(BF16)
    return slopes, bias, gmat


def _ret_tables():
    C = RET_CHUNK
    scale = R_DK ** -0.5
    gamma = 1.0 - 2.0 ** (-5.0 - jnp.arange(R_HEADS, dtype=F32))
    log_g = jnp.log(gamma)
    pos = jnp.arange(C, dtype=F32)
    diff = pos[:, None] - pos[None, :]
    dec = jnp.where(diff[None] >= 0,
                    jnp.exp(jnp.maximum(diff, 0.0)[None] * log_g[:, None, None]), 0.0) * scale
    qdec = jnp.exp((pos[None, :] + 1.0) * log_g[:, None])[..., None]
    kdec = (jnp.exp((C - 1.0 - pos[None, :]) * log_g[:, None]) * scale)[..., None]
    cdec = jnp.exp(C * log_g)
    return cdec, dec, qdec, kdec


def kernel(x, p, norm_mix, norm_mlp, norm_pe, a_w_qkv, a_w_o, a_g_q, a_g_k, a_lam_q1, a_lam_k1,
           a_lam_q2, a_lam_k2, a_g_sub, r_w_in, r_w_out, r_g_gn, mlp_w1, mlp_w2, pe_w_up,
           pe_w_gate):
    B, S, _ = x.shape
    T = B * S
    slopes, bias, gmat = _attn_tables()
    cdec, dec, qdec, kdec = _ret_tables()
    row = lambda a: a.reshape(1, -1).astype(F32)

    xt = x.reshape(T, D_MODEL)
    for i in range(DEPTH):
        j = i // N_MIXERS
        if i % N_MIXERS == 0:
            lambda_init = 0.8 - 0.6 * math.exp(-0.3 * i)
            gq_row = row(jnp.tile(a_g_q[j], D_MODEL // A_DHEAD) * (A_DHEAD ** -0.5 * LOG2E))
            gk_row = row(jnp.tile(a_g_k[j], D_MODEL // A_DHEAD))
            q, k, v = _attn_inproj(xt, row(norm_mix[i]), a_w_qkv[j].astype(BF16),
                                   gq_row, gk_row, gmat)
            shp = (B, S, D_MODEL)
            mix = _attn_core(q.reshape(shp), k.reshape(shp), v.reshape(shp), slopes, bias,
                             row(a_lam_q1[j]), row(a_lam_k1[j]), row(a_lam_q2[j]),
                             row(a_lam_k2[j]), a_g_sub[j].reshape(-1, 1).astype(F32),
                             lambda_init)
            w_out = a_w_o[j]
        else:
            proj = _ret_inproj(xt, row(norm_mix[i]), r_w_in[j].astype(BF16))
            mix = _ret_core(proj.reshape(B, S, -1), cdec, dec, qdec, kdec, row(r_g_gn[j]))
            w_out = r_w_out[j]
        xt = _tail(xt, mix.reshape(T, -1), p[i].reshape(T, PLE_DIM), w_out.astype(BF16),
                   row(norm_mlp[i]), mlp_w1[i].astype(BF16), mlp_w2[i].astype(BF16),
                   row(norm_pe[i]), pe_w_gate[i].astype(BF16), pe_w_up[i].astype(BF16))
    return xt.reshape(B, S, D_MODEL)
```

```python
import functools
import math

import jax
import jax.numpy as jnp
from jax import lax
from jax.experimental import pallas as pl
from jax.experimental.pallas import tpu as pltpu

F32 = jnp.float32
BF16 = jnp.bfloat16

EPS = 1e-6
D_MODEL = 1024
DEPTH = 4
N_MIXERS = 2
A_HEADS = 8
A_DHEAD = 64
A_VDIM = 128
R_HEADS = 4
R_DK = 256
R_DV = 512
D_FF = 4096
PLE_DIM = 256

V7X_VMEM_BYTES = 64 * 1024 * 1024
VMEM_LIMIT = V7X_VMEM_BYTES - 8 * 1024 * 1024

TOKEN_TILE = 512
ATTN_TQ = 256
ATTN_TK = 256
RET_CHUNK = 256
FF_CHUNK = 1024
NORM_GROUP_TILE = 256
NEG_BIG = -1e30
LANES = 128
ONES_ROWS = 16
LOG2E = math.log2(math.e)
SCORE_LOOKAHEAD = 2


def _resident(shape):
    nd = len(shape)
    return pl.BlockSpec(shape, lambda *_: (0,) * nd, pipeline_mode=pl.Buffered(1))


def _rms(x, g):
    return x * lax.rsqrt(jnp.mean(x * x, axis=-1, keepdims=True) + EPS) * g


def _dot(a, b):
    return jnp.dot(a, b, preferred_element_type=F32)


def _dot_nt(a, b):
    return lax.dot_general(a, b, (((1,), (1,)), ((), ())), preferred_element_type=F32)


def _dot_tn(a, b):
    return lax.dot_general(a, b, (((0,), (0,)), ((), ())), preferred_element_type=F32)


def _attn_inproj_kernel(x_ref, g_ref, wq_ref, wk_ref, wv_ref, gq_ref, gk_ref, gm_ref,
                        q_ref, k_ref, v_ref):
    xn = _rms(x_ref[...], g_ref[...]).astype(BF16)

    def qk_norm(w_ref, gain_ref, o_ref):
        y = _dot(xn, w_ref[...])
        for s in range(D_MODEL // NORM_GROUP_TILE):
            sl = slice(s * NORM_GROUP_TILE, (s + 1) * NORM_GROUP_TILE)
            ys = y[:, sl]
            ms = _dot((ys * ys).astype(BF16), gm_ref[...])
            o_ref[:, sl] = (ys * lax.rsqrt(ms + EPS) * gain_ref[:, sl]).astype(BF16)

    qk_norm(wq_ref, gq_ref, q_ref)
    qk_norm(wk_ref, gk_ref, k_ref)
    v_ref[...] = _dot(xn, wv_ref[...]).astype(BF16)


def _attn_inproj(x, g, w_qkv, gq_row, gk_row, gmat):
    T = x.shape[0]
    tm = TOKEN_TILE
    row = pl.BlockSpec((tm, D_MODEL), lambda i: (i, 0))
    out = jax.ShapeDtypeStruct((T, D_MODEL), BF16)
    wcol = lambda c: pl.BlockSpec((D_MODEL, D_MODEL), lambda i: (0, c),
                                  pipeline_mode=pl.Buffered(1))
    return pl.pallas_call(
        _attn_inproj_kernel,
        grid=(T // tm,),
        in_specs=[row, _resident((1, D_MODEL)), wcol(0), wcol(1), wcol(2),
                  _resident((1, D_MODEL)), _resident((1, D_MODEL)),
                  _resident((NORM_GROUP_TILE, NORM_GROUP_TILE))],
        out_specs=[row, row, row],
        out_shape=[out, out, out],
        compiler_params=pltpu.CompilerParams(
            dimension_semantics=("parallel",), vmem_limit_bytes=VMEM_LIMIT),
        name="attn_inproj",
    )(x, g, w_qkv, w_qkv, w_qkv, gq_row, gk_row, gmat)


def _attn_core_kernel(lambda_init, slopes_ref, q_ref, k_ref, v_ref, bias_ref,
                      lq1_ref, lk1_ref, lq2_ref, lk2_ref, gsub_ref, o_ref,
                      qcat_ref, m_ref, acc_ref, s0_ref):
    tq, tk = ATTN_TQ, ATTN_TK
    qi = pl.program_id(1)

    q = q_ref[0]
    lane = lax.broadcasted_iota(jnp.int32, (tq, A_VDIM), 1)
    for h in range(A_HEADS):
        qh = q[:, h * A_VDIM:(h + 1) * A_VDIM]
        zero = jnp.zeros_like(qh)
        qcat_ref[h, :tq, :] = jnp.where(lane < A_DHEAD, qh, zero)
        qcat_ref[h, tq:, :] = jnp.where(lane >= A_DHEAD, qh, zero)
    m_ref[...] = jnp.full_like(m_ref, NEG_BIG)
    acc_ref[...] = jnp.zeros_like(acc_ref)

    def scores(h, kj):
        k = k_ref[0, pl.ds(pl.multiple_of(kj * tk, tk), tk), h * A_VDIM:(h + 1) * A_VDIM]
        s = _dot_nt(k, qcat_ref[h])
        bias = bias_ref[jnp.where(kj == qi, A_HEADS, 0) + h]
        return jnp.concatenate([s[:, :tq] + bias, s[:, tq:] + bias], axis=1).astype(BF16)

    for a in range(SCORE_LOOKAHEAD):
        s0_ref[a] = scores(a, 0)

    def body(kj, _):
        start = pl.multiple_of(kj * tk, tk)
        dist = ((qi - kj) * tq).astype(F32)
        queue = [s0_ref[a] for a in range(SCORE_LOOKAHEAD)]
        for h in range(A_HEADS):
            s = queue.pop(0)
            ahead = h + SCORE_LOOKAHEAD
            if ahead < A_HEADS:
                queue.append(scores(ahead, kj))
            else:
                s0_ref[ahead - A_HEADS] = scores(ahead - A_HEADS, jnp.minimum(kj + 1, qi))
            v = v_ref[0, pl.ds(start, tk), h * A_VDIM:(h + 1) * A_VDIM]
            off = -slopes_ref[h] * dist
            m = m_ref[h]
            m_new, p = [], []
            for cb in range(2 * tq // LANES):
                cs = slice(cb * LANES, (cb + 1) * LANES)
                sb = s[:, cs]
                cmax = jnp.max(sb, axis=0, keepdims=True).astype(F32)
                shift = (jnp.maximum(m[:, cs], cmax + off) - off).astype(BF16)
                p.append(jnp.exp2(sb - shift))
                m_new.append(shift.astype(F32) + off)
            m_new = jnp.concatenate(m_new, axis=1)
            alpha = jnp.exp2(m - m_new)
            v_aug = jnp.concatenate([v.T, jnp.ones((ONES_ROWS, tk), BF16)], axis=0)
            acc_ref[h] = alpha * acc_ref[h] + _dot(v_aug, jnp.concatenate(p, axis=1))
            m_ref[h] = m_new
        return 0

    lax.fori_loop(0, qi + 1, body, 0)

    lam = (jnp.exp(jnp.sum(lq1_ref[...] * lk1_ref[...], keepdims=True))
           - jnp.exp(jnp.sum(lq2_ref[...] * lk2_ref[...], keepdims=True))
           + lambda_init)
    for h in range(A_HEADS):
        acc = acc_ref[h]
        a = acc[:A_VDIM] / acc[A_VDIM:A_VDIM + 1]
        o = a[:, :tq] - lam * a[:, tq:]
        o = o * lax.rsqrt(jnp.mean(o * o, axis=0, keepdims=True) + EPS) * gsub_ref[...]
        o_ref[0, :, h * A_VDIM:(h + 1) * A_VDIM] = (o * (1.0 - lambda_init)).T.astype(BF16)


def _attn_core(q, k, v, slopes, bias, lq1, lk1, lq2, lk2, gsub_col, lambda_init):
    B, S, W = q.shape
    tq, tk = ATTN_TQ, ATTN_TK
    qspec = pl.BlockSpec((1, tq, W), lambda b, i: (b, i, 0))
    kvspec = pl.BlockSpec((1, S, W), lambda b, i: (b, 0, 0))
    vec = lambda n: pl.BlockSpec((1, n), lambda b, i: (0, 0))
    return pl.pallas_call(
        functools.partial(_attn_core_kernel, lambda_init),
        grid=(B, S // tq),
        in_specs=[pl.BlockSpec(memory_space=pltpu.SMEM),
                  qspec, kvspec, kvspec, _resident(bias.shape),
                  vec(A_DHEAD), vec(A_DHEAD), vec(A_DHEAD), vec(A_DHEAD),
                  pl.BlockSpec((A_VDIM, 1), lambda b, i: (0, 0))],
        out_specs=qspec,
        out_shape=jax.ShapeDtypeStruct((B, S, W), BF16),
        scratch_shapes=[pltpu.VMEM((A_HEADS, 2 * tq, A_VDIM), BF16),
                        pltpu.VMEM((A_HEADS, 1, 2 * tq), F32),
                        pltpu.VMEM((A_HEADS, A_VDIM + ONES_ROWS, 2 * tq), F32),
                        pltpu.VMEM((SCORE_LOOKAHEAD, tk, 2 * tq), BF16)],
        compiler_params=pltpu.CompilerParams(
            dimension_semantics=("parallel", "parallel"), vmem_limit_bytes=VMEM_LIMIT),
        name="attn_core",
    )(slopes, q, k, v, bias, lq1, lk1, lq2, lk2, gsub_col)


def _ret_inproj_kernel(x_ref, g_ref, w_ref, o_ref):
    xn = _rms(x_ref[...], g_ref[...]).astype(BF16)
    n_out = o_ref.shape[1]
    for c in range(n_out // D_MODEL):
        sl = slice(c * D_MODEL, (c + 1) * D_MODEL)
        o_ref[:, sl] = _dot(xn, w_ref[:, sl]).astype(BF16)


def _ret_inproj(x, g, w):
    T = x.shape[0]
    n_out = w.shape[1]
    tm = TOKEN_TILE
    return pl.pallas_call(
        _ret_inproj_kernel,
        grid=(T // tm,),
        in_specs=[pl.BlockSpec((tm, D_MODEL), lambda i: (i, 0)),
                  _resident((1, D_MODEL)), _resident((D_MODEL, n_out))],
        out_specs=pl.BlockSpec((tm, n_out), lambda i: (i, 0)),
        out_shape=jax.ShapeDtypeStruct((T, n_out), BF16),
        compiler_params=pltpu.CompilerParams(
            dimension_semantics=("parallel",), vmem_limit_bytes=VMEM_LIMIT),
        name="ret_inproj",
    )(x, g, w)


def _ret_core_kernel(cdec_ref, proj_ref, dec_ref, qdec_ref, kdec_ref, ggn_ref, o_ref, state_ref):
    @pl.when(pl.program_id(1) == 0)
    def _():
        state_ref[...] = jnp.zeros_like(state_ref)

    k0 = R_HEADS * R_DK
    v0 = 2 * R_HEADS * R_DK
    g0 = v0 + R_HEADS * R_DV

    def scores(h):
        q = proj_ref[0, :, h * R_DK:(h + 1) * R_DK]
        k = proj_ref[0, :, k0 + h * R_DK:k0 + (h + 1) * R_DK]
        return (_dot_nt(q, k) * dec_ref[h]).astype(BF16)

    queue = [scores(a) for a in range(SCORE_LOOKAHEAD)]
    for h in range(R_HEADS):
        sc = queue.pop(0)
        if h + SCORE_LOOKAHEAD < R_HEADS:
            queue.append(scores(h + SCORE_LOOKAHEAD))
        q = proj_ref[0, :, h * R_DK:(h + 1) * R_DK]
        k = proj_ref[0, :, k0 + h * R_DK:k0 + (h + 1) * R_DK]
        v = proj_ref[0, :, v0 + h * R_DV:v0 + (h + 1) * R_DV]
        state = state_ref[h]
        o = _dot(sc, v) + _dot(q, state.astype(BF16)) * qdec_ref[h]
        kd = (k.astype(F32) * kdec_ref[h]).astype(BF16)
        state_ref[h] = cdec_ref[h] * state + _dot_tn(kd, v)
        gate = proj_ref[0, :, g0 + h * R_DV:g0 + (h + 1) * R_DV].astype(F32)
        o_ref[0, :, h * R_DV:(h + 1) * R_DV] = (
            gate * jax.nn.sigmoid(gate) * _rms(o, ggn_ref[...])).astype(BF16)


def _ret_core(proj, cdec, dec, qdec, kdec, ggn):
    B, S, W = proj.shape
    C = RET_CHUNK
    return pl.pallas_call(
        _ret_core_kernel,
        grid=(B, S // C),
        in_specs=[pl.BlockSpec(memory_space=pltpu.SMEM),
                  pl.BlockSpec((1, C, W), lambda b, n: (b, n, 0)),
                  _resident(dec.shape), _resident(qdec.shape), _resident(kdec.shape),
                  _resident((1, R_DV))],
        out_specs=pl.BlockSpec((1, C, R_HEADS * R_DV), lambda b, n: (b, n, 0)),
        out_shape=jax.ShapeDtypeStruct((B, S, R_HEADS * R_DV), BF16),
        scratch_shapes=[pltpu.VMEM((R_HEADS, R_DK, R_DV), F32)],
        compiler_params=pltpu.CompilerParams(
            dimension_semantics=("parallel", "arbitrary"), vmem_limit_bytes=VMEM_LIMIT),
        name="ret_core",
    )(cdec, proj, dec, qdec, kdec, ggn)


def _tail_kernel(x_ref, y_ref, p_ref, wo_ref, gmlp_ref, w1_ref, w2_ref, gpe_ref, wg_ref,
                 wup_ref, o_ref):
    x = x_ref[...] + _dot(y_ref[...], wo_ref[...])
    xn = _rms(x, gmlp_ref[...]).astype(BF16)
    acc = x
    for c in range(D_FF // FF_CHUNK):
        sl = slice(c * FF_CHUNK, (c + 1) * FF_CHUNK)
        hid = jnp.maximum(_dot(xn, w1_ref[:, sl]), 0.0)
        acc = acc + _dot((hid * hid).astype(BF16), w2_ref[sl, :])
    x = acc
    gate = jax.nn.sigmoid(_dot(_rms(x, gpe_ref[...]).astype(BF16), wg_ref[...]))
    o_ref[...] = x + gate * _dot(p_ref[...].astype(BF16), wup_ref[...])


def _tail(x, y, p, layer, wo, gmlp, w1, w2, gpe, wg, wup):
    T = x.shape[0]
    dy = y.shape[1]
    tm = TOKEN_TILE
    row = pl.BlockSpec((tm, D_MODEL), lambda i: (i, 0))
    return pl.pallas_call(
        _tail_kernel,
        grid=(T // tm,),
        in_specs=[row,
                  pl.BlockSpec((tm, dy), lambda i: (i, 0)),
                  pl.BlockSpec((None, tm, PLE_DIM), lambda i: (layer, i, 0)),
                  _resident((dy, D_MODEL)), _resident((1, D_MODEL)),
                  _resident((D_MODEL, D_FF)), _resident((D_FF, D_MODEL)),
                  _resident((1, D_MODEL)), _resident((D_MODEL, D_MODEL)),
                  _resident((PLE_DIM, D_MODEL))],
        out_specs=row,
        out_shape=jax.ShapeDtypeStruct((T, D_MODEL), F32),
        compiler_params=pltpu.CompilerParams(
            dimension_semantics=("parallel",), vmem_limit_bytes=VMEM_LIMIT),
        name="layer_tail",
    )(x, y, p, wo, gmlp, w1, w2, gpe, wg, wup)


def _attn_tables():
    slopes = jnp.array([2.0 ** (-8.0 * (h + 1) / A_HEADS) for h in range(A_HEADS)], F32)
    kpos = jnp.arange(ATTN_TK, dtype=F32)[:, None]
    qpos = jnp.arange(ATTN_TQ, dtype=F32)[None, :]
    slopes = slopes * LOG2E
    bias = -slopes[:, None, None] * (qpos - kpos)[None]
    bias = jnp.concatenate([bias, jnp.where((qpos >= kpos)[None], bias, NEG_BIG)], axis=0)
    gmat = jnp.where(jnp.arange(NORM_GROUP_TILE)[:, None] // A_DHEAD
                     == jnp.arange(NORM_GROUP_TILE)[None, :] // A_DHEAD,
                     1.0 / A_DHEAD, 0.0).astype(BF16)
    return slopes, bias, gmat


def _ret_tables():
    C = RET_CHUNK
    scale = R_DK ** -0.5
    gamma = 1.0 - 2.0 ** (-5.0 - jnp.arange(R_HEADS, dtype=F32))
    log_g = jnp.log(gamma)
    pos = jnp.arange(C, dtype=F32)
    diff = pos[:, None] - pos[None, :]
    dec = jnp.where(diff[None] >= 0,
                    jnp.exp(jnp.maximum(diff, 0.0)[None] * log_g[:, None, None]), 0.0) * scale
    qdec = jnp.exp((pos[None, :] + 1.0) * log_g[:, None])[..., None]
    kdec = (jnp.exp((C - 1.0 - pos[None, :]) * log_g[:, None]) * scale)[..., None]
    cdec = jnp.exp(C * log_g)
    return cdec, dec, qdec, kdec


def kernel(x, p, norm_mix, norm_mlp, norm_pe, a_w_qkv, a_w_o, a_g_q, a_g_k, a_lam_q1, a_lam_k1,
           a_lam_q2, a_lam_k2, a_g_sub, r_w_in, r_w_out, r_g_gn, mlp_w1, mlp_w2, pe_w_up,
           pe_w_gate):
    B, S, _ = x.shape
    T = B * S
    slopes, bias, gmat = _attn_tables()
    cdec, dec, qdec, kdec = _ret_tables()
    row = lambda a: a.reshape(1, -1).astype(F32)

    xt = x.reshape(T, D_MODEL)
    for i in range(DEPTH):
        j = i // N_MIXERS
        if i % N_MIXERS == 0:
            lambda_init = 0.8 - 0.6 * math.exp(-0.3 * i)
            gq_row = row(jnp.tile(a_g_q[j], D_MODEL // A_DHEAD) * (A_DHEAD ** -0.5 * LOG2E))
            gk_row = row(jnp.tile(a_g_k[j], D_MODEL // A_DHEAD))
            q, k, v = _attn_inproj(xt, row(norm_mix[i]), a_w_qkv[j].astype(BF16),
                                   gq_row, gk_row, gmat)
            shp = (B, S, D_MODEL)
            mix = _attn_core(q.reshape(shp), k.reshape(shp), v.reshape(shp), slopes, bias,
                             row(a_lam_q1[j]), row(a_lam_k1[j]), row(a_lam_q2[j]),
                             row(a_lam_k2[j]), a_g_sub[j].reshape(-1, 1).astype(F32),
                             lambda_init)
            w_out = a_w_o[j]
        else:
            proj = _ret_inproj(xt, row(norm_mix[i]), r_w_in[j].astype(BF16))
            mix = _ret_core(proj.reshape(B, S, -1), cdec, dec, qdec, kdec, row(r_g_gn[j]))
            w_out = r_w_out[j]
        xt = _tail(xt, mix.reshape(T, -1), p.reshape(DEPTH, T, PLE_DIM), i, w_out.astype(BF16),
                   row(norm_mlp[i]), mlp_w1[i].astype(BF16), mlp_w2[i].astype(BF16),
                   row(norm_pe[i]), pe_w_gate[i].astype(BF16), pe_w_up[i].astype(BF16))
    return xt.reshape(B, S, D_MODEL)
```

```python
import functools
import math

import jax
import jax.numpy as jnp
from jax import lax
from jax.experimental import pallas as pl
from jax.experimental.pallas import tpu as pltpu

F32 = jnp.float32
BF16 = jnp.bfloat16

EPS = 1e-6
D_MODEL = 1024
DEPTH = 4
N_MIXERS = 2
A_HEADS = 8
A_DHEAD = 64
A_VDIM = 128
R_HEADS = 4
R_DK = 256
R_DV = 512
D_FF = 4096
PLE_DIM = 256

V7X_VMEM_BYTES = 64 * 1024 * 1024
VMEM_LIMIT = V7X_VMEM_BYTES - 8 * 1024 * 1024

TOKEN_TILE = 512
ATTN_TQ = 256
ATTN_TK = 256
RET_CHUNK = 256
FF_CHUNK = 1024
NORM_GROUP_TILE = 256
NEG_BIG = -1e30
LANES = 128
ONES_ROWS = 16
LOG2E = math.log2(math.e)
SCORE_LOOKAHEAD = 2


def _resident(shape):
    nd = len(shape)
    return pl.BlockSpec(shape, lambda *_: (0,) * nd, pipeline_mode=pl.Buffered(1))


def _resident_layer(stacked, layer):
    _, rows, cols = stacked.shape
    return pl.BlockSpec((None, rows, cols), lambda *_: (layer, 0, 0),
                        pipeline_mode=pl.Buffered(1))


def _rms(x, g):
    return x * lax.rsqrt(jnp.mean(x * x, axis=-1, keepdims=True) + EPS) * g


def _dot(a, b):
    return jnp.dot(a, b, preferred_element_type=F32)


def _dot_nt(a, b):
    return lax.dot_general(a, b, (((1,), (1,)), ((), ())), preferred_element_type=F32)


def _dot_tn(a, b):
    return lax.dot_general(a, b, (((0,), (0,)), ((), ())), preferred_element_type=F32)


def _attn_inproj_kernel(x_ref, g_ref, wq_ref, wk_ref, wv_ref, gq_ref, gk_ref, gm_ref,
                        q_ref, k_ref, v_ref):
    xn = _rms(x_ref[...], g_ref[...]).astype(BF16)

    def qk_norm(w_ref, gain_ref, o_ref):
        y = _dot(xn, w_ref[...])
        for s in range(D_MODEL // NORM_GROUP_TILE):
            sl = slice(s * NORM_GROUP_TILE, (s + 1) * NORM_GROUP_TILE)
            ys = y[:, sl]
            ms = _dot((ys * ys).astype(BF16), gm_ref[...])
            o_ref[:, sl] = (ys * lax.rsqrt(ms + EPS) * gain_ref[:, sl]).astype(BF16)

    qk_norm(wq_ref, gq_ref, q_ref)
    qk_norm(wk_ref, gk_ref, k_ref)
    v_ref[...] = _dot(xn, wv_ref[...]).astype(BF16)


def _attn_inproj(x, g, w_qkv, layer, gq_row, gk_row, gmat):
    T = x.shape[0]
    tm = TOKEN_TILE
    row = pl.BlockSpec((tm, D_MODEL), lambda i: (i, 0))
    out = jax.ShapeDtypeStruct((T, D_MODEL), BF16)
    wcol = lambda c: pl.BlockSpec((None, D_MODEL, D_MODEL), lambda i: (layer, 0, c),
                                  pipeline_mode=pl.Buffered(1))
    return pl.pallas_call(
        _attn_inproj_kernel,
        grid=(T // tm,),
        in_specs=[row, _resident((1, D_MODEL)), wcol(0), wcol(1), wcol(2),
                  _resident((1, D_MODEL)), _resident((1, D_MODEL)),
                  _resident((NORM_GROUP_TILE, NORM_GROUP_TILE))],
        out_specs=[row, row, row],
        out_shape=[out, out, out],
        compiler_params=pltpu.CompilerParams(
            dimension_semantics=("parallel",), vmem_limit_bytes=VMEM_LIMIT),
        name="attn_inproj",
    )(x, g, w_qkv, w_qkv, w_qkv, gq_row, gk_row, gmat)


def _attn_core_kernel(lambda_init, slopes_ref, q_ref, k_ref, v_ref, bias_ref,
                      lq1_ref, lk1_ref, lq2_ref, lk2_ref, gsub_ref, o_ref,
                      qcat_ref, m_ref, acc_ref, s0_ref):
    tq, tk = ATTN_TQ, ATTN_TK
    qi = pl.program_id(1)

    q = q_ref[0]
    lane = lax.broadcasted_iota(jnp.int32, (tq, A_VDIM), 1)
    for h in range(A_HEADS):
        qh = q[:, h * A_VDIM:(h + 1) * A_VDIM]
        zero = jnp.zeros_like(qh)
        qcat_ref[h, :tq, :] = jnp.where(lane < A_DHEAD, qh, zero)
        qcat_ref[h, tq:, :] = jnp.where(lane >= A_DHEAD, qh, zero)
    m_ref[...] = jnp.full_like(m_ref, NEG_BIG)
    acc_ref[...] = jnp.zeros_like(acc_ref)

    def scores(h, kj):
        k = k_ref[0, pl.ds(pl.multiple_of(kj * tk, tk), tk), h * A_VDIM:(h + 1) * A_VDIM]
        s = _dot_nt(k, qcat_ref[h])
        bias = bias_ref[jnp.where(kj == qi, A_HEADS, 0) + h]
        return jnp.concatenate([s[:, :tq] + bias, s[:, tq:] + bias], axis=1).astype(BF16)

    for a in range(SCORE_LOOKAHEAD):
        s0_ref[a] = scores(a, 0)

    def body(kj, _):
        start = pl.multiple_of(kj * tk, tk)
        dist = ((qi - kj) * tq).astype(F32)
        queue = [s0_ref[a] for a in range(SCORE_LOOKAHEAD)]
        for h in range(A_HEADS):
            s = queue.pop(0)
            ahead = h + SCORE_LOOKAHEAD
            if ahead < A_HEADS:
                queue.append(scores(ahead, kj))
            else:
                s0_ref[ahead - A_HEADS] = scores(ahead - A_HEADS, jnp.minimum(kj + 1, qi))
            v = v_ref[0, pl.ds(start, tk), h * A_VDIM:(h + 1) * A_VDIM]
            off = -slopes_ref[h] * dist
            m = m_ref[h]
            m_new, p = [], []
            for cb in range(2 * tq // LANES):
                cs = slice(cb * LANES, (cb + 1) * LANES)
                sb = s[:, cs]
                cmax = jnp.max(sb, axis=0, keepdims=True).astype(F32)
                shift = (jnp.maximum(m[:, cs], cmax + off) - off).astype(BF16)
                p.append(jnp.exp2(sb - shift))
                m_new.append(shift.astype(F32) + off)
            m_new = jnp.concatenate(m_new, axis=1)
            alpha = jnp.exp2(m - m_new)
            v_aug = jnp.concatenate([v.T, jnp.ones((ONES_ROWS, tk), BF16)], axis=0)
            acc_ref[h] = alpha * acc_ref[h] + _dot(v_aug, jnp.concatenate(p, axis=1))
            m_ref[h] = m_new
        return 0

    lax.fori_loop(0, qi + 1, body, 0)

    lam = (jnp.exp(jnp.sum(lq1_ref[...] * lk1_ref[...], keepdims=True))
           - jnp.exp(jnp.sum(lq2_ref[...] * lk2_ref[...], keepdims=True))
           + lambda_init)
    for h in range(A_HEADS):
        acc = acc_ref[h]
        a = acc[:A_VDIM] / acc[A_VDIM:A_VDIM + 1]
        o = a[:, :tq] - lam * a[:, tq:]
        o = o * lax.rsqrt(jnp.mean(o * o, axis=0, keepdims=True) + EPS) * gsub_ref[...]
        o_ref[0, :, h * A_VDIM:(h + 1) * A_VDIM] = (o * (1.0 - lambda_init)).T.astype(BF16)


def _attn_core(q, k, v, slopes, bias, lq1, lk1, lq2, lk2, gsub_col, lambda_init):
    B, S, W = q.shape
    tq, tk = ATTN_TQ, ATTN_TK
    qspec = pl.BlockSpec((1, tq, W), lambda b, i: (b, i, 0))
    kvspec = pl.BlockSpec((1, S, W), lambda b, i: (b, 0, 0))
    vec = lambda n: pl.BlockSpec((1, n), lambda b, i: (0, 0))
    return pl.pallas_call(
        functools.partial(_attn_core_kernel, lambda_init),
        grid=(B, S // tq),
        in_specs=[pl.BlockSpec(memory_space=pltpu.SMEM),
                  qspec, kvspec, kvspec, _resident(bias.shape),
                  vec(A_DHEAD), vec(A_DHEAD), vec(A_DHEAD), vec(A_DHEAD),
                  pl.BlockSpec((A_VDIM, 1), lambda b, i: (0, 0))],
        out_specs=qspec,
        out_shape=jax.ShapeDtypeStruct((B, S, W), BF16),
        scratch_shapes=[pltpu.VMEM((A_HEADS, 2 * tq, A_VDIM), BF16),
                        pltpu.VMEM((A_HEADS, 1, 2 * tq), F32),
                        pltpu.VMEM((A_HEADS, A_VDIM + ONES_ROWS, 2 * tq), F32),
                        pltpu.VMEM((SCORE_LOOKAHEAD, tk, 2 * tq), BF16)],
        compiler_params=pltpu.CompilerParams(
            dimension_semantics=("parallel", "parallel"), vmem_limit_bytes=VMEM_LIMIT),
        name="attn_core",
    )(slopes, q, k, v, bias, lq1, lk1, lq2, lk2, gsub_col)


def _ret_inproj_kernel(x_ref, g_ref, w_ref, o_ref):
    xn = _rms(x_ref[...], g_ref[...]).astype(BF16)
    n_out = o_ref.shape[1]
    for c in range(n_out // D_MODEL):
        sl = slice(c * D_MODEL, (c + 1) * D_MODEL)
        o_ref[:, sl] = _dot(xn, w_ref[:, sl]).astype(BF16)


def _ret_inproj(x, g, w, layer):
    T = x.shape[0]
    n_out = w.shape[2]
    tm = TOKEN_TILE
    return pl.pallas_call(
        _ret_inproj_kernel,
        grid=(T // tm,),
        in_specs=[pl.BlockSpec((tm, D_MODEL), lambda i: (i, 0)),
                  _resident((1, D_MODEL)), _resident_layer(w, layer)],
        out_specs=pl.BlockSpec((tm, n_out), lambda i: (i, 0)),
        out_shape=jax.ShapeDtypeStruct((T, n_out), BF16),
        compiler_params=pltpu.CompilerParams(
            dimension_semantics=("parallel",), vmem_limit_bytes=VMEM_LIMIT),
        name="ret_inproj",
    )(x, g, w)


def _ret_core_kernel(cdec_ref, proj_ref, dec_ref, qdec_ref, kdec_ref, ggn_ref, o_ref, state_ref):
    @pl.when(pl.program_id(1) == 0)
    def _():
        state_ref[...] = jnp.zeros_like(state_ref)

    k0 = R_HEADS * R_DK
    v0 = 2 * R_HEADS * R_DK
    g0 = v0 + R_HEADS * R_DV

    def scores(h):
        q = proj_ref[0, :, h * R_DK:(h + 1) * R_DK]
        k = proj_ref[0, :, k0 + h * R_DK:k0 + (h + 1) * R_DK]
        return (_dot_nt(q, k) * dec_ref[h]).astype(BF16)

    queue = [scores(a) for a in range(SCORE_LOOKAHEAD)]
    for h in range(R_HEADS):
        sc = queue.pop(0)
        if h + SCORE_LOOKAHEAD < R_HEADS:
            queue.append(scores(h + SCORE_LOOKAHEAD))
        q = proj_ref[0, :, h * R_DK:(h + 1) * R_DK]
        k = proj_ref[0, :, k0 + h * R_DK:k0 + (h + 1) * R_DK]
        v = proj_ref[0, :, v0 + h * R_DV:v0 + (h + 1) * R_DV]
        state = state_ref[h]
        o = _dot(sc, v) + _dot(q, state.astype(BF16)) * qdec_ref[h]
        kd = (k.astype(F32) * kdec_ref[h]).astype(BF16)
        state_ref[h] = cdec_ref[h] * state + _dot_tn(kd, v)
        gate = proj_ref[0, :, g0 + h * R_DV:g0 + (h + 1) * R_DV].astype(F32)
        o_ref[0, :, h * R_DV:(h + 1) * R_DV] = (
            gate * jax.nn.sigmoid(gate) * _rms(o, ggn_ref[...])).astype(BF16)


def _ret_core(proj, cdec, dec, qdec, kdec, ggn):
    B, S, W = proj.shape
    C = RET_CHUNK
    return pl.pallas_call(
        _ret_core_kernel,
        grid=(B, S // C),
        in_specs=[pl.BlockSpec(memory_space=pltpu.SMEM),
                  pl.BlockSpec((1, C, W), lambda b, n: (b, n, 0)),
                  _resident(dec.shape), _resident(qdec.shape), _resident(kdec.shape),
                  _resident((1, R_DV))],
        out_specs=pl.BlockSpec((1, C, R_HEADS * R_DV), lambda b, n: (b, n, 0)),
        out_shape=jax.ShapeDtypeStruct((B, S, R_HEADS * R_DV), BF16),
        scratch_shapes=[pltpu.VMEM((R_HEADS, R_DK, R_DV), F32)],
        compiler_params=pltpu.CompilerParams(
            dimension_semantics=("parallel", "arbitrary"), vmem_limit_bytes=VMEM_LIMIT),
        name="ret_core",
    )(cdec, proj, dec, qdec, kdec, ggn)


def _tail_kernel(x_ref, y_ref, p_ref, wo_ref, gmlp_ref, w1_ref, w2_ref, gpe_ref, wg_ref,
                 wup_ref, o_ref):
    x = x_ref[...] + _dot(y_ref[...], wo_ref[...])
    xn = _rms(x, gmlp_ref[...]).astype(BF16)
    acc = x
    for c in range(D_FF // FF_CHUNK):
        sl = slice(c * FF_CHUNK, (c + 1) * FF_CHUNK)
        hid = jnp.maximum(_dot(xn, w1_ref[:, sl]), 0.0)
        acc = acc + _dot((hid * hid).astype(BF16), w2_ref[sl, :])
    x = acc
    gate = jax.nn.sigmoid(_dot(_rms(x, gpe_ref[...]).astype(BF16), wg_ref[...]))
    o_ref[...] = x + gate * _dot(p_ref[...].astype(BF16), wup_ref[...])


def _tail(x, y, p, layer, wo, wo_layer, gmlp, w1, w2, gpe, wg, wup):
    T = x.shape[0]
    dy = y.shape[1]
    tm = TOKEN_TILE
    row = pl.BlockSpec((tm, D_MODEL), lambda i: (i, 0))
    return pl.pallas_call(
        _tail_kernel,
        grid=(T // tm,),
        in_specs=[row,
                  pl.BlockSpec((tm, dy), lambda i: (i, 0)),
                  pl.BlockSpec((None, tm, PLE_DIM), lambda i: (layer, i, 0)),
                  _resident_layer(wo, wo_layer), _resident((1, D_MODEL)),
                  _resident_layer(w1, layer), _resident_layer(w2, layer),
                  _resident((1, D_MODEL)), _resident_layer(wg, layer),
                  _resident_layer(wup, layer)],
        out_specs=row,
        out_shape=jax.ShapeDtypeStruct((T, D_MODEL), F32),
        compiler_params=pltpu.CompilerParams(
            dimension_semantics=("parallel",), vmem_limit_bytes=VMEM_LIMIT),
        name="layer_tail",
    )(x, y, p, wo, gmlp, w1, w2, gpe, wg, wup)


def _attn_tables():
    slopes = jnp.array([2.0 ** (-8.0 * (h + 1) / A_HEADS) for h in range(A_HEADS)], F32)
    kpos = jnp.arange(ATTN_TK, dtype=F32)[:, None]
    qpos = jnp.arange(ATTN_TQ, dtype=F32)[None, :]
    slopes = slopes * LOG2E
    bias = -slopes[:, None, None] * (qpos - kpos)[None]
    bias = jnp.concatenate([bias, jnp.where((qpos >= kpos)[None], bias, NEG_BIG)], axis=0)
    gmat = jnp.where(jnp.arange(NORM_GROUP_TILE)[:, None] // A_DHEAD
                     == jnp.arange(NORM_GROUP_TILE)[None, :] // A_DHEAD,
                     1.0 / A_DHEAD, 0.0).astype(BF16)
    return slopes, bias, gmat


def _ret_tables():
    C = RET_CHUNK
    scale = R_DK ** -0.5
    gamma = 1.0 - 2.0 ** (-5.0 - jnp.arange(R_HEADS, dtype=F32))
    log_g = jnp.log(gamma)
    pos = jnp.arange(C, dtype=F32)
    diff = pos[:, None] - pos[None, :]
    dec = jnp.where(diff[None] >= 0,
                    jnp.exp(jnp.maximum(diff, 0.0)[None] * log_g[:, None, None]), 0.0) * scale
    qdec = jnp.exp((pos[None, :] + 1.0) * log_g[:, None])[..., None]
    kdec = (jnp.exp((C - 1.0 - pos[None, :]) * log_g[:, None]) * scale)[..., None]
    cdec = jnp.exp(C * log_g)
    return cdec, dec, qdec, kdec


def kernel(x, p, norm_mix, norm_mlp, norm_pe, a_w_qkv, a_w_o, a_g_q, a_g_k, a_lam_q1, a_lam_k1,
           a_lam_q2, a_lam_k2, a_g_sub, r_w_in, r_w_out, r_g_gn, mlp_w1, mlp_w2, pe_w_up,
           pe_w_gate):
    B, S, _ = x.shape
    T = B * S
    slopes, bias, gmat = _attn_tables()
    cdec, dec, qdec, kdec = _ret_tables()
    row = lambda a: a.reshape(1, -1).astype(F32)

    bf = lambda a: a.astype(BF16)
    a_w_qkv, a_w_o, r_w_in, r_w_out = bf(a_w_qkv), bf(a_w_o), bf(r_w_in), bf(r_w_out)
    mlp_w1, mlp_w2, pe_w_gate, pe_w_up = bf(mlp_w1), bf(mlp_w2), bf(pe_w_gate), bf(pe_w_up)

    xt = x.reshape(T, D_MODEL)
    for i in range(DEPTH):
        j = i // N_MIXERS
        if i % N_MIXERS == 0:
            lambda_init = 0.8 - 0.6 * math.exp(-0.3 * i)
            gq_row = row(jnp.tile(a_g_q[j], D_MODEL // A_DHEAD) * (A_DHEAD ** -0.5 * LOG2E))
            gk_row = row(jnp.tile(a_g_k[j], D_MODEL // A_DHEAD))
            q, k, v = _attn_inproj(xt, row(norm_mix[i]), a_w_qkv, j, gq_row, gk_row, gmat)
            shp = (B, S, D_MODEL)
            mix = _attn_core(q.reshape(shp), k.reshape(shp), v.reshape(shp), slopes, bias,
                             row(a_lam_q1[j]), row(a_lam_k1[j]), row(a_lam_q2[j]),
                             row(a_lam_k2[j]), a_g_sub[j].reshape(-1, 1).astype(F32),
                             lambda_init)
            w_out = a_w_o
        else:
            proj = _ret_inproj(xt, row(norm_mix[i]), r_w_in, j)
            mix = _ret_core(proj.reshape(B, S, -1), cdec, dec, qdec, kdec, row(r_g_gn[j]))
            w_out = r_w_out
        xt = _tail(xt, mix.reshape(T, -1), p.reshape(DEPTH, T, PLE_DIM), i, w_out, j,
                   row(norm_mlp[i]), mlp_w1, mlp_w2, row(norm_pe[i]), pe_w_gate, pe_w_up)
    return xt.reshape(B, S, D_MODEL)
```

```python
import functools
import math

import jax
import jax.numpy as jnp
from jax import lax
from jax.experimental import pallas as pl
from jax.experimental.pallas import tpu as pltpu

F32 = jnp.float32
BF16 = jnp.bfloat16

EPS = 1e-6
D_MODEL = 1024
DEPTH = 4
N_MIXERS = 2
A_HEADS = 8
A_DHEAD = 64
A_VDIM = 128
R_HEADS = 4
R_DK = 256
R_DV = 512
D_FF = 4096
PLE_DIM = 256

V7X_VMEM_BYTES = 64 * 1024 * 1024
VMEM_LIMIT = V7X_VMEM_BYTES - 8 * 1024 * 1024

TOKEN_TILE = 512
ATTN_TQ = 256
ATTN_TK = 256
RET_CHUNK = 256
FF_CHUNK = 1024
NORM_GROUP_TILE = 256
NEG_BIG = -1e30
LANES = 128
ONES_ROWS = 16
LOG2E = math.log2(math.e)
SCORE_LOOKAHEAD = 3


def _resident(shape):
    nd = len(shape)
    return pl.BlockSpec(shape, lambda *_: (0,) * nd, pipeline_mode=pl.Buffered(1))


def _resident_layer(stacked, layer):
    _, rows, cols = stacked.shape
    return pl.BlockSpec((None, rows, cols), lambda *_: (layer, 0, 0),
                        pipeline_mode=pl.Buffered(1))


def _rms(x, g):
    return x * lax.rsqrt(jnp.mean(x * x, axis=-1, keepdims=True) + EPS) * g


def _dot(a, b):
    return jnp.dot(a, b, preferred_element_type=F32)


def _dot_nt(a, b):
    return lax.dot_general(a, b, (((1,), (1,)), ((), ())), preferred_element_type=F32)


def _dot_tn(a, b):
    return lax.dot_general(a, b, (((0,), (0,)), ((), ())), preferred_element_type=F32)


def _attn_inproj_kernel(x_ref, g_ref, wq_ref, wk_ref, wv_ref, gq_ref, gk_ref, gm_ref,
                        q_ref, k_ref, v_ref):
    xn = _rms(x_ref[...], g_ref[...]).astype(BF16)

    def qk_norm(w_ref, gain_ref, o_ref):
        y = _dot(xn, w_ref[...])
        for s in range(D_MODEL // NORM_GROUP_TILE):
            sl = slice(s * NORM_GROUP_TILE, (s + 1) * NORM_GROUP_TILE)
            ys = y[:, sl]
            ms = _dot((ys * ys).astype(BF16), gm_ref[...])
            o_ref[:, sl] = (ys * lax.rsqrt(ms + EPS) * gain_ref[:, sl]).astype(BF16)

    qk_norm(wq_ref, gq_ref, q_ref)
    qk_norm(wk_ref, gk_ref, k_ref)
    v_ref[...] = _dot(xn, wv_ref[...]).astype(BF16)


def _attn_inproj(x, g, w_qkv, layer, gq_row, gk_row, gmat):
    T = x.shape[0]
    tm = TOKEN_TILE
    row = pl.BlockSpec((tm, D_MODEL), lambda i: (i, 0))
    out = jax.ShapeDtypeStruct((T, D_MODEL), BF16)
    wcol = lambda c: pl.BlockSpec((None, D_MODEL, D_MODEL), lambda i: (layer, 0, c),
                                  pipeline_mode=pl.Buffered(1))
    return pl.pallas_call(
        _attn_inproj_kernel,
        grid=(T // tm,),
        in_specs=[row, _resident((1, D_MODEL)), wcol(0), wcol(1), wcol(2),
                  _resident((1, D_MODEL)), _resident((1, D_MODEL)),
                  _resident((NORM_GROUP_TILE, NORM_GROUP_TILE))],
        out_specs=[row, row, row],
        out_shape=[out, out, out],
        compiler_params=pltpu.CompilerParams(
            dimension_semantics=("parallel",), vmem_limit_bytes=VMEM_LIMIT),
        name="attn_inproj",
    )(x, g, w_qkv, w_qkv, w_qkv, gq_row, gk_row, gmat)


def _attn_core_kernel(lambda_init, slopes_ref, q_ref, k_ref, v_ref, bias_ref,
                      lq1_ref, lk1_ref, lq2_ref, lk2_ref, gsub_ref, o_ref,
                      qcat_ref, m_ref, acc_ref, s0_ref):
    tq, tk = ATTN_TQ, ATTN_TK
    qi = pl.program_id(1)

    q = q_ref[0]
    lane = lax.broadcasted_iota(jnp.int32, (tq, A_VDIM), 1)
    for h in range(A_HEADS):
        qh = q[:, h * A_VDIM:(h + 1) * A_VDIM]
        zero = jnp.zeros_like(qh)
        qcat_ref[h, :tq, :] = jnp.where(lane < A_DHEAD, qh, zero)
        qcat_ref[h, tq:, :] = jnp.where(lane >= A_DHEAD, qh, zero)
    m_ref[...] = jnp.full_like(m_ref, NEG_BIG)
    acc_ref[...] = jnp.zeros_like(acc_ref)

    def scores(h, kj):
        k = k_ref[0, pl.ds(pl.multiple_of(kj * tk, tk), tk), h * A_VDIM:(h + 1) * A_VDIM]
        s = _dot_nt(k, qcat_ref[h])
        bias = bias_ref[jnp.where(kj == qi, A_HEADS, 0) + h]
        return jnp.concatenate([s[:, :tq] + bias, s[:, tq:] + bias], axis=1).astype(BF16)

    for a in range(SCORE_LOOKAHEAD):
        s0_ref[a] = scores(a, 0)

    def body(kj, _):
        start = pl.multiple_of(kj * tk, tk)
        dist = ((qi - kj) * tq).astype(F32)
        queue = [s0_ref[a] for a in range(SCORE_LOOKAHEAD)]
        for h in range(A_HEADS):
            s = queue.pop(0)
            ahead = h + SCORE_LOOKAHEAD
            if ahead < A_HEADS:
                queue.append(scores(ahead, kj))
            else:
                s0_ref[ahead - A_HEADS] = scores(ahead - A_HEADS, jnp.minimum(kj + 1, qi))
            v = v_ref[0, pl.ds(start, tk), h * A_VDIM:(h + 1) * A_VDIM]
            off = -slopes_ref[h] * dist
            m = m_ref[h]
            m_new, p = [], []
            for cb in range(2 * tq // LANES):
                cs = slice(cb * LANES, (cb + 1) * LANES)
                sb = s[:, cs]
                cmax = jnp.max(sb, axis=0, keepdims=True).astype(F32)
                shift = (jnp.maximum(m[:, cs], cmax + off) - off).astype(BF16)
                p.append(jnp.exp2(sb - shift))
                m_new.append(shift.astype(F32) + off)
            m_new = jnp.concatenate(m_new, axis=1)
            alpha = jnp.exp2(m - m_new)
            v_aug = jnp.concatenate([v.T, jnp.ones((ONES_ROWS, tk), BF16)], axis=0)
            acc_ref[h] = alpha * acc_ref[h] + _dot(v_aug, jnp.concatenate(p, axis=1))
            m_ref[h] = m_new
        return 0

    lax.fori_loop(0, qi + 1, body, 0)

    lam = (jnp.exp(jnp.sum(lq1_ref[...] * lk1_ref[...], keepdims=True))
           - jnp.exp(jnp.sum(lq2_ref[...] * lk2_ref[...], keepdims=True))
           + lambda_init)
    for h in range(A_HEADS):
        acc = acc_ref[h]
        a = acc[:A_VDIM] / acc[A_VDIM:A_VDIM + 1]
        o = a[:, :tq] - lam * a[:, tq:]
        o = o * lax.rsqrt(jnp.mean(o * o, axis=0, keepdims=True) + EPS) * gsub_ref[...]
        o_ref[0, :, h * A_VDIM:(h + 1) * A_VDIM] = (o * (1.0 - lambda_init)).T.astype(BF16)


def _attn_core(q, k, v, slopes, bias, lq1, lk1, lq2, lk2, gsub_col, lambda_init):
    B, S, W = q.shape
    tq, tk = ATTN_TQ, ATTN_TK
    qspec = pl.BlockSpec((1, tq, W), lambda b, i: (b, i, 0))
    kvspec = pl.BlockSpec((1, S, W), lambda b, i: (b, 0, 0))
    vec = lambda n: pl.BlockSpec((1, n), lambda b, i: (0, 0))
    return pl.pallas_call(
        functools.partial(_attn_core_kernel, lambda_init),
        grid=(B, S // tq),
        in_specs=[pl.BlockSpec(memory_space=pltpu.SMEM),
                  qspec, kvspec, kvspec, _resident(bias.shape),
                  vec(A_DHEAD), vec(A_DHEAD), vec(A_DHEAD), vec(A_DHEAD),
                  pl.BlockSpec((A_VDIM, 1), lambda b, i: (0, 0))],
        out_specs=qspec,
        out_shape=jax.ShapeDtypeStruct((B, S, W), BF16),
        scratch_shapes=[pltpu.VMEM((A_HEADS, 2 * tq, A_VDIM), BF16),
                        pltpu.VMEM((A_HEADS, 1, 2 * tq), F32),
                        pltpu.VMEM((A_HEADS, A_VDIM + ONES_ROWS, 2 * tq), F32),
                        pltpu.VMEM((SCORE_LOOKAHEAD, tk, 2 * tq), BF16)],
        compiler_params=pltpu.CompilerParams(
            dimension_semantics=("parallel", "parallel"), vmem_limit_bytes=VMEM_LIMIT),
        name="attn_core",
    )(slopes, q, k, v, bias, lq1, lk1, lq2, lk2, gsub_col)


def _ret_inproj_kernel(x_ref, g_ref, w_ref, o_ref):
    xn = _rms(x_ref[...], g_ref[...]).astype(BF16)
    n_out = o_ref.shape[1]
    for c in range(n_out // D_MODEL):
        sl = slice(c * D_MODEL, (c + 1) * D_MODEL)
        o_ref[:, sl] = _dot(xn, w_ref[:, sl]).astype(BF16)


def _ret_inproj(x, g, w, layer):
    T = x.shape[0]
    n_out = w.shape[2]
    tm = TOKEN_TILE
    return pl.pallas_call(
        _ret_inproj_kernel,
        grid=(T // tm,),
        in_specs=[pl.BlockSpec((tm, D_MODEL), lambda i: (i, 0)),
                  _resident((1, D_MODEL)), _resident_layer(w, layer)],
        out_specs=pl.BlockSpec((tm, n_out), lambda i: (i, 0)),
        out_shape=jax.ShapeDtypeStruct((T, n_out), BF16),
        compiler_params=pltpu.CompilerParams(
            dimension_semantics=("parallel",), vmem_limit_bytes=VMEM_LIMIT),
        name="ret_inproj",
    )(x, g, w)


def _ret_core_kernel(cdec_ref, proj_ref, dec_ref, qdec_ref, kdec_ref, ggn_ref, o_ref, state_ref):
    @pl.when(pl.program_id(1) == 0)
    def _():
        state_ref[...] = jnp.zeros_like(state_ref)

    k0 = R_HEADS * R_DK
    v0 = 2 * R_HEADS * R_DK
    g0 = v0 + R_HEADS * R_DV

    def scores(h):
        q = proj_ref[0, :, h * R_DK:(h + 1) * R_DK]
        k = proj_ref[0, :, k0 + h * R_DK:k0 + (h + 1) * R_DK]
        return (_dot_nt(q, k) * dec_ref[h]).astype(BF16)

    queue = [scores(a) for a in range(SCORE_LOOKAHEAD)]
    for h in range(R_HEADS):
        sc = queue.pop(0)
        if h + SCORE_LOOKAHEAD < R_HEADS:
            queue.append(scores(h + SCORE_LOOKAHEAD))
        q = proj_ref[0, :, h * R_DK:(h + 1) * R_DK]
        k = proj_ref[0, :, k0 + h * R_DK:k0 + (h + 1) * R_DK]
        v = proj_ref[0, :, v0 + h * R_DV:v0 + (h + 1) * R_DV]
        state = state_ref[h]
        o = _dot(sc, v) + _dot(q, state.astype(BF16)) * qdec_ref[h]
        kd = (k.astype(F32) * kdec_ref[h]).astype(BF16)
        state_ref[h] = cdec_ref[h] * state + _dot_tn(kd, v)
        gate = proj_ref[0, :, g0 + h * R_DV:g0 + (h + 1) * R_DV].astype(F32)
        o_ref[0, :, h * R_DV:(h + 1) * R_DV] = (
            gate * jax.nn.sigmoid(gate) * _rms(o, ggn_ref[...])).astype(BF16)


def _ret_core(proj, cdec, dec, qdec, kdec, ggn):
    B, S, W = proj.shape
    C = RET_CHUNK
    return pl.pallas_call(
        _ret_core_kernel,
        grid=(B, S // C),
        in_specs=[pl.BlockSpec(memory_space=pltpu.SMEM),
                  pl.BlockSpec((1, C, W), lambda b, n: (b, n, 0)),
                  _resident(dec.shape), _resident(qdec.shape), _resident(kdec.shape),
                  _resident((1, R_DV))],
        out_specs=pl.BlockSpec((1, C, R_HEADS * R_DV), lambda b, n: (b, n, 0)),
        out_shape=jax.ShapeDtypeStruct((B, S, R_HEADS * R_DV), BF16),
        scratch_shapes=[pltpu.VMEM((R_HEADS, R_DK, R_DV), F32)],
        compiler_params=pltpu.CompilerParams(
            dimension_semantics=("parallel", "arbitrary"), vmem_limit_bytes=VMEM_LIMIT),
        name="ret_core",
    )(cdec, proj, dec, qdec, kdec, ggn)


def _tail_kernel(x_ref, y_ref, p_ref, wo_ref, gmlp_ref, w1_ref, w2_ref, gpe_ref, wg_ref,
                 wup_ref, o_ref):
    x = x_ref[...] + _dot(y_ref[...], wo_ref[...])
    xn = _rms(x, gmlp_ref[...]).astype(BF16)
    acc = x
    for c in range(D_FF // FF_CHUNK):
        sl = slice(c * FF_CHUNK, (c + 1) * FF_CHUNK)
        hid = jnp.maximum(_dot(xn, w1_ref[:, sl]), 0.0)
        acc = acc + _dot((hid * hid).astype(BF16), w2_ref[sl, :])
    x = acc
    gate = jax.nn.sigmoid(_dot(_rms(x, gpe_ref[...]).astype(BF16), wg_ref[...]))
    o_ref[...] = x + gate * _dot(p_ref[...].astype(BF16), wup_ref[...])


def _tail(x, y, p, layer, wo, wo_layer, gmlp, w1, w2, gpe, wg, wup):
    T = x.shape[0]
    dy = y.shape[1]
    tm = TOKEN_TILE
    row = pl.BlockSpec((tm, D_MODEL), lambda i: (i, 0))
    return pl.pallas_call(
        _tail_kernel,
        grid=(T // tm,),
        in_specs=[row,
                  pl.BlockSpec((tm, dy), lambda i: (i, 0)),
                  pl.BlockSpec((None, tm, PLE_DIM), lambda i: (layer, i, 0)),
                  _resident_layer(wo, wo_layer), _resident((1, D_MODEL)),
                  _resident_layer(w1, layer), _resident_layer(w2, layer),
                  _resident((1, D_MODEL)), _resident_layer(wg, layer),
                  _resident_layer(wup, layer)],
        out_specs=row,
        out_shape=jax.ShapeDtypeStruct((T, D_MODEL), F32),
        compiler_params=pltpu.CompilerParams(
            dimension_semantics=("parallel",), vmem_limit_bytes=VMEM_LIMIT),
        name="layer_tail",
    )(x, y, p, wo, gmlp, w1, w2, gpe, wg, wup)


def _attn_tables():
    slopes = jnp.array([2.0 ** (-8.0 * (h + 1) / A_HEADS) for h in range(A_HEADS)], F32)
    kpos = jnp.arange(ATTN_TK, dtype=F32)[:, None]
    qpos = jnp.arange(ATTN_TQ, dtype=F32)[None, :]
    slopes = slopes * LOG2E
    bias = -slopes[:, None, None] * (qpos - kpos)[None]
    bias = jnp.concatenate([bias, jnp.where((qpos >= kpos)[None], bias, NEG_BIG)], axis=0)
    gmat = jnp.where(jnp.arange(NORM_GROUP_TILE)[:, None] // A_DHEAD
                     == jnp.arange(NORM_GROUP_TILE)[None, :] // A_DHEAD,
                     1.0 / A_DHEAD, 0.0).astype(BF16)
    return slopes, bias, gmat


def _ret_tables():
    C = RET_CHUNK
    scale = R_DK ** -0.5
    gamma = 1.0 - 2.0 ** (-5.0 - jnp.arange(R_HEADS, dtype=F32))
    log_g = jnp.log(gamma)
    pos = jnp.arange(C, dtype=F32)
    diff = pos[:, None] - pos[None, :]
    dec = jnp.where(diff[None] >= 0,
                    jnp.exp(jnp.maximum(diff, 0.0)[None] * log_g[:, None, None]), 0.0) * scale
    qdec = jnp.exp((pos[None, :] + 1.0) * log_g[:, None])[..., None]
    kdec = (jnp.exp((C - 1.0 - pos[None, :]) * log_g[:, None]) * scale)[..., None]
    cdec = jnp.exp(C * log_g)
    return cdec, dec, qdec, kdec


def kernel(x, p, norm_mix, norm_mlp, norm_pe, a_w_qkv, a_w_o, a_g_q, a_g_k, a_lam_q1, a_lam_k1,
           a_lam_q2, a_lam_k2, a_g_sub, r_w_in, r_w_out, r_g_gn, mlp_w1, mlp_w2, pe_w_up,
           pe_w_gate):
    B, S, _ = x.shape
    T = B * S
    slopes, bias, gmat = _attn_tables()
    cdec, dec, qdec, kdec = _ret_tables()
    row = lambda a: a.reshape(1, -1).astype(F32)

    bf = lambda a: a.astype(BF16)
    a_w_qkv, a_w_o, r_w_in, r_w_out = bf(a_w_qkv), bf(a_w_o), bf(r_w_in), bf(r_w_out)
    mlp_w1, mlp_w2, pe_w_gate, pe_w_up = bf(mlp_w1), bf(mlp_w2), bf(pe_w_gate), bf(pe_w_up)

    xt = x.reshape(T, D_MODEL)
    for i in range(DEPTH):
        j = i // N_MIXERS
        if i % N_MIXERS == 0:
            lambda_init = 0.8 - 0.6 * math.exp(-0.3 * i)
            gq_row = row(jnp.tile(a_g_q[j], D_MODEL // A_DHEAD) * (A_DHEAD ** -0.5 * LOG2E))
            gk_row = row(jnp.tile(a_g_k[j], D_MODEL // A_DHEAD))
            q, k, v = _attn_inproj(xt, row(norm_mix[i]), a_w_qkv, j, gq_row, gk_row, gmat)
            shp = (B, S, D_MODEL)
            mix = _attn_core(q.reshape(shp), k.reshape(shp), v.reshape(shp), slopes, bias,
                             row(a_lam_q1[j]), row(a_lam_k1[j]), row(a_lam_q2[j]),
                             row(a_lam_k2[j]), a_g_sub[j].reshape(-1, 1).astype(F32),
                             lambda_init)
            w_out = a_w_o
        else:
            proj = _ret_inproj(xt, row(norm_mix[i]), r_w_in, j)
            mix = _ret_core(proj.reshape(B, S, -1), cdec, dec, qdec, kdec, row(r_g_gn[j]))
            w_out = r_w_out
        xt = _tail(xt, mix.reshape(T, -1), p.reshape(DEPTH, T, PLE_DIM), i, w_out, j,
                   row(norm_mlp[i]), mlp_w1, mlp_w2, row(norm_pe[i]), pe_w_gate, pe_w_up)
    return xt.reshape(B, S, D_MODEL)
```
